```python
import math
import jax, jax.numpy as jnp
from jax import lax
import numpy as np

D_MODEL = 4096
BATCH = 4
SEQ = 2048
DEPTH = 4

CHUNK = 64
N_MIXERS = 2
MEM_TOKENS = 256
X_HEADS = 4
X_WIDTH = D_MODEL // 4
X_HEAD_DIM = X_WIDTH // X_HEADS
MIX_WIDTH = D_MODEL - X_WIDTH
SC_WIDTH = 3
GDN_CONV_WIDTH = 4
GDN_HEAD_DIM = 128
GDN_HEADS = MIX_WIDTH // GDN_HEAD_DIM
D_FF = 3 * D_MODEL // 2
EPS = 1e-6
SC_IN = 3 * MIX_WIDTH + X_WIDTH
GDN_IN = 4 * MIX_WIDTH + 2 * GDN_HEADS + X_WIDTH

kernel_name = "chunk_causal_hybrid_conv_gdn_trunk"


def rms_norm(x, g):
    xf = x.astype(jnp.float32)
    y = xf * lax.rsqrt(jnp.mean(xf * xf, axis=-1, keepdims=True) + EPS)
    return (y * g.astype(jnp.float32)).astype(x.dtype)


def l2_norm(x):
    xf = x.astype(jnp.float32)
    return xf * lax.rsqrt(jnp.sum(xf * xf, axis=-1, keepdims=True) + EPS)


def swiglu(x, w_gate, w_up, w_down):
    return (jax.nn.silu(x @ w_gate) * (x @ w_up)) @ w_down


def macaron_ffn(h, pre_g, post_g, w_gate, w_up, w_down):
    return h + 0.5 * rms_norm(swiglu(rms_norm(h, pre_g), w_gate, w_up, w_down), post_g)


def causal_depthwise_conv(x, w):
    width = w.shape[0]
    seq = x.shape[1]
    xp = jnp.pad(x, ((0, 0), (width - 1, 0), (0, 0)))
    return sum(xp[:, j:j + seq] * w[j] for j in range(width))


def memory_cross_attention(xq, mem_n, w_kv):
    b, s, _ = xq.shape
    m = mem_n.shape[1]
    k, v = jnp.split(mem_n @ w_kv, 2, axis=-1)
    q = xq.reshape(b, s, X_HEADS, X_HEAD_DIM)
    k = k.reshape(b, m, X_HEADS, X_HEAD_DIM)
    v = v.reshape(b, m, X_HEADS, X_HEAD_DIM)
    scores = jnp.einsum('bshd,bmhd->bhsm', q, k).astype(jnp.float32) * (X_HEAD_DIM ** -0.5)
    p = jax.nn.softmax(scores, axis=-1).astype(v.dtype)
    return jnp.einsum('bhsm,bmhd->bshd', p, v).reshape(b, s, X_WIDTH)


def short_conv_mixer(hn, mem_n, w_in, conv_w, w_kv, w_out):
    proj = hn @ w_in
    gate_b, gate_c, hid, xq = jnp.split(proj, [MIX_WIDTH, 2 * MIX_WIDTH, 3 * MIX_WIDTH], axis=-1)
    y = gate_b * causal_depthwise_conv(gate_c * hid, conv_w)
    xo = memory_cross_attention(xq, mem_n, w_kv)
    return jnp.concatenate([y, xo], axis=-1) @ w_out


def chunk_gated_delta_rule(q, k, v, g, beta):
    bsz, s, h, dk = q.shape
    dv = v.shape[-1]
    n = s // CHUNK
    f32 = jnp.float32

    def to_chunks(t):
        t = jnp.moveaxis(t.astype(f32), 2, 1)
        return t.reshape(t.shape[:2] + (n, CHUNK) + t.shape[3:])

    q, k, v, g, beta = (to_chunks(t) for t in (q, k, v, g, beta))
    q = q * (dk ** -0.5)
    k_beta = k * beta[..., None]
    v_beta = v * beta[..., None]
    g = jnp.cumsum(g, axis=-1)
    causal = jnp.tril(jnp.ones((CHUNK, CHUNK), dtype=bool))
    strict = jnp.tril(jnp.ones((CHUNK, CHUNK), dtype=bool), k=-1)
    diff = g[..., :, None] - g[..., None, :]
    decay = jnp.where(causal, jnp.exp(jnp.where(causal, diff, 0.0)), 0.0)
    lower = jnp.where(strict, jnp.einsum('bhncd,bhnsd->bhncs', k_beta, k) * decay, 0.0)
    eye = jnp.eye(CHUNK, dtype=f32)
    t_mat = lax.linalg.triangular_solve(eye + lower, jnp.broadcast_to(eye, lower.shape),
                                        left_side=True, lower=True, unit_diagonal=True)
    u = t_mat @ v_beta
    w = t_mat @ (k_beta * jnp.exp(g)[..., None])
    intra = jnp.where(causal, jnp.einsum('bhncd,bhnsd->bhncs', q, k) * decay, 0.0)
    q_dec = q * jnp.exp(g)[..., None]
    k_tail = k * jnp.exp(g[..., -1:] - g)[..., None]
    chunk_decay = jnp.exp(g[..., -1])

    def step(state, inp):
        w_c, u_c, q_c, intra_c, k_c, d_c = inp
        v_new = u_c - w_c @ state
        out = q_c @ state + intra_c @ v_new
        state = state * d_c[..., None, None] + jnp.einsum('bhck,bhcv->bhkv', k_c, v_new)
        return state, out

    xs = tuple(jnp.moveaxis(t, 2, 0) for t in (w, u, q_dec, intra, k_tail, chunk_decay))
    state0 = jnp.zeros((bsz, h, dk, dv), f32)
    _, out = lax.scan(step, state0, xs)
    out = jnp.moveaxis(out, 0, 2).reshape(bsz, h, s, dv)
    return jnp.moveaxis(out, 1, 2)


def gated_deltanet_mixer(hn, mem_n, w_in, conv_w, a_log, dt_bias, norm_g, w_kv, w_out):
    b, s, _ = hn.shape
    f32 = jnp.float32
    proj = hn @ w_in
    qkv, z, beta_in, a_in, xq = jnp.split(
        proj, [3 * MIX_WIDTH, 4 * MIX_WIDTH, 4 * MIX_WIDTH + GDN_HEADS, 4 * MIX_WIDTH + 2 * GDN_HEADS], axis=-1)
    qkv = jax.nn.silu(causal_depthwise_conv(qkv, conv_w))
    q, k, v = jnp.split(qkv, 3, axis=-1)
    q = l2_norm(q.reshape(b, s, GDN_HEADS, GDN_HEAD_DIM))
    k = l2_norm(k.reshape(b, s, GDN_HEADS, GDN_HEAD_DIM))
    v = v.reshape(b, s, GDN_HEADS, GDN_HEAD_DIM).astype(f32)
    beta = jax.nn.sigmoid(beta_in.astype(f32))
    g = -jnp.exp(a_log.astype(f32)) * jax.nn.softplus(a_in.astype(f32) + dt_bias.astype(f32))
    o = chunk_gated_delta_rule(q, k, v, g, beta)
    o = o * lax.rsqrt(jnp.mean(o * o, axis=-1, keepdims=True) + EPS) * norm_g.astype(f32)
    o = o * jax.nn.silu(z.reshape(b, s, GDN_HEADS, GDN_HEAD_DIM).astype(f32))
    y = o.reshape(b, s, MIX_WIDTH).astype(hn.dtype)
    xo = memory_cross_attention(xq, mem_n, w_kv)
    return jnp.concatenate([y, xo], axis=-1) @ w_out


def setup_inputs(seed: int = 0) -> dict:
    key = jax.random.key(seed)
    ks = iter(jax.random.split(key, 24))
    f32 = jnp.float32
    n_a = (DEPTH + N_MIXERS - 1) // N_MIXERS
    n_b = DEPTH // N_MIXERS

    def dense(shape, fan_in):
        return jax.random.normal(next(ks), shape, f32) * (fan_in ** -0.5)

    def gain(shape):
        return 1.0 + 0.02 * jax.random.normal(next(ks), shape, f32)

    x = jax.random.normal(next(ks), (BATCH, SEQ, D_MODEL), f32)
    mem = jax.random.normal(next(ks), (BATCH, MEM_TOKENS, D_MODEL), f32)
    ffn_pre_g = gain((DEPTH, 2, D_MODEL))
    ffn_post_g = gain((DEPTH, 2, D_MODEL))
    mix_pre_g = gain((DEPTH, D_MODEL))
    mix_post_g = gain((DEPTH, D_MODEL))
    mem_g = gain((DEPTH, D_MODEL))
    ffn_w_gate = dense((DEPTH, 2, D_MODEL, D_FF), D_MODEL)
    ffn_w_up = dense((DEPTH, 2, D_MODEL, D_FF), D_MODEL)
    ffn_w_down = dense((DEPTH, 2, D_FF, D_MODEL), D_FF)
    mem_w_kv = dense((DEPTH, D_MODEL, 2 * X_WIDTH), D_MODEL)
    mix_w_out = dense((DEPTH, D_MODEL, D_MODEL), D_MODEL)
    sc_w_in = dense((n_a, D_MODEL, SC_IN), D_MODEL)
    sc_conv_w = dense((n_a, SC_WIDTH, MIX_WIDTH), SC_WIDTH)
    gdn_w_in = dense((n_b, D_MODEL, GDN_IN), D_MODEL)
    gdn_conv_w = dense((n_b, GDN_CONV_WIDTH, 3 * MIX_WIDTH), GDN_CONV_WIDTH)
    gdn_a_log = jnp.log(jax.random.uniform(next(ks), (n_b, GDN_HEADS), f32, 1.0, 16.0))
    dt = jnp.exp(jax.random.uniform(next(ks), (n_b, GDN_HEADS), f32, math.log(1e-3), math.log(1e-1)))
    gdn_dt_bias = dt + jnp.log(-jnp.expm1(-dt))
    gdn_norm_g = gain((n_b, GDN_HEAD_DIM))
    return {"x": x, "mem": mem, "ffn_pre_g": ffn_pre_g, "ffn_post_g": ffn_post_g,
            "mix_pre_g": mix_pre_g, "mix_post_g": mix_post_g, "mem_g": mem_g,
            "ffn_w_gate": ffn_w_gate, "ffn_w_up": ffn_w_up, "ffn_w_down": ffn_w_down,
            "mem_w_kv": mem_w_kv, "mix_w_out": mix_w_out, "sc_w_in": sc_w_in, "sc_conv_w": sc_conv_w,
            "gdn_w_in": gdn_w_in, "gdn_conv_w": gdn_conv_w, "gdn_a_log": gdn_a_log,
            "gdn_dt_bias": gdn_dt_bias, "gdn_norm_g": gdn_norm_g}


def reference(x, mem, ffn_pre_g, ffn_post_g, mix_pre_g, mix_post_g, mem_g, ffn_w_gate, ffn_w_up,
              ffn_w_down, mem_w_kv, mix_w_out, sc_w_in, sc_conv_w, gdn_w_in, gdn_conv_w, gdn_a_log,
              gdn_dt_bias, gdn_norm_g):
    h = x
    for i in range(DEPTH):
        j = i // N_MIXERS
        h = macaron_ffn(h, ffn_pre_g[i, 0], ffn_post_g[i, 0], ffn_w_gate[i, 0], ffn_w_up[i, 0], ffn_w_down[i, 0])
        hn = rms_norm(h, mix_pre_g[i])
        mem_n = rms_norm(mem, mem_g[i])
        if i % N_MIXERS == 0:
            mixed = short_conv_mixer(hn, mem_n, sc_w_in[j], sc_conv_w[j], mem_w_kv[i], mix_w_out[i])
        else:
            mixed = gated_deltanet_mixer(hn, mem_n, gdn_w_in[j], gdn_conv_w[j], gdn_a_log[j],
                                         gdn_dt_bias[j], gdn_norm_g[j], mem_w_kv[i], mix_w_out[i])
        h = h + rms_norm(mixed, mix_post_g[i])
        h = macaron_ffn(h, ffn_pre_g[i, 1], ffn_post_g[i, 1], ffn_w_gate[i, 1], ffn_w_up[i, 1], ffn_w_down[i, 1])
    return h
```

```python
import functools

import jax
import jax.numpy as jnp
from jax import lax
from jax.experimental import pallas as pl
from jax.experimental.pallas import tpu as pltpu

CHUNK = 64
N_MIXERS = 2
X_HEADS = 4
GDN_HEAD_DIM = 128
EPS = 1e-6
LANES = 128
V7X_VMEM_BYTES = 64 * 2**20
VMEM_LIMIT_BYTES = V7X_VMEM_BYTES - 6 * 2**20

F32 = jnp.float32
BF16 = jnp.bfloat16
NT = (((1,), (1,)), ((), ()))
TN = (((0,), (0,)), ((), ()))


def _tile(n, prefs):
    for p in prefs:
        if n % p == 0:
            return p
    return n


def _params(*semantics):
    return pltpu.CompilerParams(dimension_semantics=semantics, vmem_limit_bytes=VMEM_LIMIT_BYTES)


def _rms(x, g):
    return x * lax.rsqrt(jnp.mean(x * x, axis=-1, keepdims=True) + EPS) * g


def _silu(x):
    return x * jax.nn.sigmoid(x)


def _shift_rows(x, s):
    if s == 0:
        return x
    row = lax.broadcasted_iota(jnp.int32, x.shape, 0)
    return jnp.where(row >= s, pltpu.roll(x, s, 0), 0.0)


def _causal_conv(x, w):
    width = w.shape[0]
    acc = None
    for j in range(width):
        term = _shift_rows(x, width - 1 - j) * w[j:j + 1, :]
        acc = term if acc is None else acc + term
    return acc


def _norm_kernel(x_ref, g_ref, o_ref):
    o_ref[...] = _rms(x_ref[...], g_ref[...]).astype(o_ref.dtype)


def _norm_bf16(x, g):
    t, d = x.shape
    tr = _tile(t, (256, 128, 64, 32, 16))
    return pl.pallas_call(
        _norm_kernel,
        out_shape=jax.ShapeDtypeStruct((t, d), BF16),
        grid=(t // tr,),
        in_specs=[pl.BlockSpec((tr, d), lambda i: (i, 0)), pl.BlockSpec((1, d), lambda i: (0, 0))],
        out_specs=pl.BlockSpec((tr, d), lambda i: (i, 0)),
        compiler_params=_params("parallel"),
        name="norm",
    )(x, g.reshape(1, d))


def _resid_kernel(y_ref, h_ref, pg_ref, ng_ref, h_out, hn_out, *, weight):
    hnew = h_ref[...] + weight * _rms(y_ref[...], pg_ref[...])
    h_out[...] = hnew
    hn_out[...] = _rms(hnew, ng_ref[...]).astype(hn_out.dtype)


def _resid_last_kernel(y_ref, h_ref, pg_ref, h_out, *, weight):
    h_out[...] = h_ref[...] + weight * _rms(y_ref[...], pg_ref[...])


def _residual(y, h, post_g, next_g, weight):
    t, d = h.shape
    tr = _tile(t, (256, 128, 64, 32, 16))
    row = pl.BlockSpec((tr, d), lambda i: (i, 0))
    vec = pl.BlockSpec((1, d), lambda i: (0, 0))
    if next_g is None:
        return pl.pallas_call(
            functools.partial(_resid_last_kernel, weight=weight),
            out_shape=jax.ShapeDtypeStruct((t, d), F32),
            grid=(t // tr,), in_specs=[row, row, vec], out_specs=row,
            compiler_params=_params("parallel"), name="residual_last",
        )(y, h, post_g.reshape(1, d)), None
    return pl.pallas_call(
        functools.partial(_resid_kernel, weight=weight),
        out_shape=(jax.ShapeDtypeStruct((t, d), F32), jax.ShapeDtypeStruct((t, d), BF16)),
        grid=(t // tr,), in_specs=[row, row, vec, vec], out_specs=(row, row),
        compiler_params=_params("parallel"), name="residual",
    )(y, h, post_g.reshape(1, d), next_g.reshape(1, d))


def _linear_kernel(*refs, n_x):
    x_refs, w_ref, o_ref = refs[:n_x], refs[n_x], refs[n_x + 1]
    acc, k0 = None, 0
    for x_ref in x_refs:
        kk = x_ref.shape[1]
        part = jnp.dot(x_ref[...], w_ref[k0:k0 + kk, :].astype(BF16), preferred_element_type=F32)
        acc = part if acc is None else acc + part
        k0 += kk
    o_ref[...] = acc.astype(o_ref.dtype)


def _linear(xs, w, lead, col0, ncols, out_dtype, name):
    t = xs[0].shape[0]
    k = sum(x.shape[1] for x in xs)
    assert w.shape[-2] == k
    tm = _tile(t, (1024, 512, 256, 128, 64, 32, 16))
    tn = _tile(ncols, (512, 256, 128))
    assert col0 % tn == 0
    cb = col0 // tn
    nlead = len(lead)
    w_spec = pl.BlockSpec((None,) * nlead + (k, tn), lambda i, j: tuple(lead) + (0, cb + j))
    x_specs = [pl.BlockSpec((tm, x.shape[1]), lambda i, j: (i, 0)) for x in xs]
    return pl.pallas_call(
        functools.partial(_linear_kernel, n_x=len(xs)),
        out_shape=jax.ShapeDtypeStruct((t, ncols), out_dtype),
        grid=(t // tm, ncols // tn),
        in_specs=x_specs + [w_spec],
        out_specs=pl.BlockSpec((tm, tn), lambda i, j: (i, j)),
        compiler_params=_params("parallel", "arbitrary"),
        name=name,
    )(*xs, w)


def _swiglu_up_kernel(x_ref, wg_ref, wu_ref, o_ref):
    x = x_ref[...]
    gate = jnp.dot(x, wg_ref[...].astype(BF16), preferred_element_type=F32)
    up = jnp.dot(x, wu_ref[...].astype(BF16), preferred_element_type=F32)
    o_ref[...] = (_silu(gate) * up).astype(o_ref.dtype)


def _swiglu_up(x, w_gate, w_up, lead):
    t, k = x.shape
    f = w_gate.shape[-1]
    tm = _tile(t, (1024, 512, 256, 128, 64, 32, 16))
    tn = _tile(f, (256, 128))
    w_spec = pl.BlockSpec((None,) * len(lead) + (k, tn), lambda i, j: tuple(lead) + (0, j))
    return pl.pallas_call(
        _swiglu_up_kernel,
        out_shape=jax.ShapeDtypeStruct((t, f), BF16),
        grid=(t // tm, f // tn),
        in_specs=[pl.BlockSpec((tm, k), lambda i, j: (i, 0)), w_spec, w_spec],
        out_specs=pl.BlockSpec((tm, tn), lambda i, j: (i, j)),
        compiler_params=_params("parallel", "arbitrary"),
        name="swiglu_up",
    )(x, w_gate, w_up)


def _sconv_kernel(b_ref, c_ref, h_ref, w_ref, o_ref):
    u = c_ref[...] * h_ref[...]
    o_ref[...] = (b_ref[...] * _causal_conv(u, w_ref[...])).astype(o_ref.dtype)


def _short_conv(proj, conv_w, mix):
    b, s, _ = proj.shape
    tc = _tile(mix, (256, 128))
    nb = mix // tc
    blk = lambda off: pl.BlockSpec((None, s, tc), lambda bi, j: (bi, 0, off + j))
    return pl.pallas_call(
        _sconv_kernel,
        out_shape=jax.ShapeDtypeStruct((b, s, mix), BF16),
        grid=(b, nb),
        in_specs=[blk(0), blk(nb), blk(2 * nb), pl.BlockSpec((conv_w.shape[0], tc), lambda bi, j: (0, j))],
        out_specs=pl.BlockSpec((None, s, tc), lambda bi, j: (bi, 0, j)),
        compiler_params=_params("parallel", "parallel"),
        name="short_conv",
    )(proj, proj, proj, conv_w)


def _xattn_kernel(q_ref, k_ref, v_ref, o_ref):
    dh = q_ref.shape[-1] // X_HEADS
    scale = dh ** -0.5
    for hd in range(X_HEADS):
        sl = slice(hd * dh, (hd + 1) * dh)
        q = q_ref[:, sl].astype(BF16)
        k = k_ref[:, sl].astype(BF16)
        v = v_ref[:, sl].astype(BF16)
        s = lax.dot_general(q, k, NT, preferred_element_type=F32) * scale
        e = jnp.exp(s - jnp.max(s, axis=-1, keepdims=True))
        p = e / jnp.sum(e, axis=-1, keepdims=True)
        o_ref[:, sl] = jnp.dot(p.astype(BF16), v, preferred_element_type=F32).astype(o_ref.dtype)


def _cross_attention(q_arr, q_blk, kv, xw):
    b, s, _ = q_arr.shape
    m = kv.shape[1]
    tq = _tile(s, (512, 256, 128, 64))
    return pl.pallas_call(
        _xattn_kernel,
        out_shape=jax.ShapeDtypeStruct((b, s, xw), BF16),
        grid=(b, s // tq),
        in_specs=[pl.BlockSpec((None, tq, xw), lambda bi, i: (bi, i, q_blk)),
                  pl.BlockSpec((None, m, xw), lambda bi, i: (bi, 0, 0)),
                  pl.BlockSpec((None, m, xw), lambda bi, i: (bi, 0, 1))],
        out_specs=pl.BlockSpec((None, tq, xw), lambda bi, i: (bi, i, 0)),
        compiler_params=_params("parallel", "parallel"),
        name="cross_attention",
    )(q_arr, kv, kv)


def _gdn_prep_kernel(q_ref, k_ref, v_ref, wq_ref, wk_ref, wv_ref, b_ref, a_ref, alog_ref, dtb_ref,
                     qd_ref, kt_ref, vb_ref, kbg_ref, intra_ref, l_ref, cd_ref,
                     qs, ks, vs, gc_s, bt_s):
    dk = q_ref.shape[-1]
    nchunks = b_ref.shape[0]

    def l2n(x):
        return x * lax.rsqrt(jnp.sum(x * x, axis=-1, keepdims=True) + EPS)

    qs[...] = l2n(_silu(_causal_conv(q_ref[...], wq_ref[...]))) * (dk ** -0.5)
    ks[...] = l2n(_silu(_causal_conv(k_ref[...], wk_ref[...])))
    vs[...] = _silu(_causal_conv(v_ref[...], wv_ref[...]))

    bt_s[...] = jax.nn.sigmoid(b_ref[...])
    x = a_ref[...] + dtb_ref[...]
    softplus = jnp.maximum(x, 0.0) + jnp.log1p(jnp.exp(-jnp.abs(x)))
    g = -jnp.exp(alog_ref[...]) * softplus
    ii = lax.broadcasted_iota(jnp.int32, (CHUNK, CHUNK), 0)
    jj = lax.broadcasted_iota(jnp.int32, (CHUNK, CHUNK), 1)
    upper = (ii <= jj).astype(F32)
    gc_s[...] = jnp.dot(g, upper, precision=lax.Precision.HIGHEST, preferred_element_type=F32)

    eye, tri, strict = ii == jj, ii >= jj, ii > jj

    def chunk(n, carry):
        r0 = pl.multiple_of(n * CHUNK, CHUNK)
        rows = pl.ds(r0, CHUNK)
        qn, kn, vn = qs[rows, :], ks[rows, :], vs[rows, :]
        g_row = gc_s[pl.ds(n, 1), :]
        b_row = bt_s[pl.ds(n, 1), :]
        g_col = jnp.sum(jnp.where(eye, g_row, 0.0), axis=1, keepdims=True)
        b_col = jnp.sum(jnp.where(eye, b_row, 0.0), axis=1, keepdims=True)
        decay = jnp.where(tri, jnp.exp(jnp.where(tri, g_col - g_row, 0.0)), 0.0)
        kb = kn * b_col
        k16 = kn.astype(BF16)
        kk = lax.dot_general(kb.astype(BF16), k16, NT, preferred_element_type=F32)
        l_ref[rows, :] = jnp.where(strict, kk * decay, 0.0)
        qk = lax.dot_general(qn.astype(BF16), k16, NT, preferred_element_type=F32)
        intra_ref[rows, :] = jnp.where(tri, qk * decay, 0.0).astype(intra_ref.dtype)
        eg = jnp.exp(g_col)
        g_last = g_row[:, CHUNK - 1:CHUNK]
        qd_ref[rows, :] = (qn * eg).astype(qd_ref.dtype)
        kbg_ref[rows, :] = (kb * eg).astype(kbg_ref.dtype)
        kt_ref[rows, :] = (kn * jnp.exp(g_last - g_col)).astype(kt_ref.dtype)
        vb_ref[rows, :] = (vn * b_col).astype(vb_ref.dtype)
        cd_ref[pl.ds(n, 1), :] = jnp.broadcast_to(jnp.exp(g_last), (1, cd_ref.shape[1]))
        return carry

    lax.fori_loop(0, nchunks, chunk, 0)


def _solve_kernel(l_ref, t_ref):
    c = l_ref.shape[0]
    col_id = lax.broadcasted_iota(jnp.int32, (c, t_ref.shape[2]), 0)
    t_ref[0] = jnp.where(col_id == 0, 1.0, 0.0)

    def row(r, carry):
        def term(m, acc):
            return acc - l_ref[r, pl.ds(m, 1), :] * t_ref[m]
        t_ref[r] = lax.fori_loop(0, r, term, jnp.where(col_id == r, 1.0, 0.0))
        return carry

    lax.fori_loop(1, c, row, 0)


def _gdn_scan_kernel(qd_ref, kt_ref, vb_ref, kbg_ref, intra_ref, t_ref, cd_ref, z_ref, ng_ref, y_ref):
    nchunks = cd_ref.shape[0]
    dk, dv = qd_ref.shape[-1], vb_ref.shape[-1]

    def chunk(n, state):
        r0 = pl.multiple_of(n * CHUNK, CHUNK)
        rows = pl.ds(r0, CHUNK)
        t16 = t_ref[rows, :].astype(BF16)
        u = jnp.dot(t16, vb_ref[rows, :], preferred_element_type=F32)
        w = jnp.dot(t16, kbg_ref[rows, :], preferred_element_type=F32)
        s16 = state.astype(BF16)
        v_new = u - jnp.dot(w.astype(BF16), s16, preferred_element_type=F32)
        v16 = v_new.astype(BF16)
        o = (jnp.dot(qd_ref[rows, :], s16, preferred_element_type=F32)
             + jnp.dot(intra_ref[rows, :], v16, preferred_element_type=F32))
        state = state * cd_ref[pl.ds(n, 1), :] + lax.dot_general(kt_ref[rows, :], v16, TN,
                                                                  preferred_element_type=F32)
        o = o * lax.rsqrt(jnp.mean(o * o, axis=-1, keepdims=True) + EPS) * ng_ref[...]
        y_ref[rows, :] = (o * _silu(z_ref[rows, :])).astype(y_ref.dtype)
        return state

    lax.fori_loop(0, nchunks, chunk, jnp.zeros((dk, dv), F32))


def _gated_deltanet(proj, ba, conv_w, a_log, dt_bias, norm_g, heads):
    b, s, _ = proj.shape
    d = GDN_HEAD_DIM
    nchunks = s // CHUNK
    ba_t = jnp.swapaxes(ba[:, :, :2 * heads], 1, 2).reshape(b, 2 * heads, nchunks, CHUNK)
    alog_b = jnp.broadcast_to(a_log.reshape(heads, 1, 1), (heads, 1, CHUNK))
    dtb_b = jnp.broadcast_to(dt_bias.reshape(heads, 1, 1), (heads, 1, CHUNK))

    seq = lambda off: pl.BlockSpec((None, s, d), lambda bi, h: (bi, 0, off + h))
    cw = lambda off: pl.BlockSpec((conv_w.shape[0], d), lambda bi, h: (0, off + h))
    gate = lambda off: pl.BlockSpec((None, None, nchunks, CHUNK), lambda bi, h: (bi, off + h, 0, 0))
    hvec = pl.BlockSpec((None, 1, CHUNK), lambda bi, h: (h, 0, 0))
    per_head = lambda width: pl.BlockSpec((None, None, s, width), lambda bi, h: (bi, h, 0, 0))
    hs = lambda width, dt: jax.ShapeDtypeStruct((b, heads, s, width), dt)

    qd, kt, vb, kbg, intra, lmat, cd = pl.pallas_call(
        _gdn_prep_kernel,
        out_shape=(hs(d, BF16), hs(d, BF16), hs(d, BF16), hs(d, BF16), hs(CHUNK, BF16), hs(CHUNK, F32),
                   jax.ShapeDtypeStruct((b, heads, nchunks, LANES), F32)),
        grid=(b, heads),
        in_specs=[seq(0), seq(heads), seq(2 * heads), cw(0), cw(heads), cw(2 * heads),
                  gate(0), gate(heads), hvec, hvec],
        out_specs=(per_head(d), per_head(d), per_head(d), per_head(d), per_head(CHUNK), per_head(CHUNK),
                   pl.BlockSpec((None, None, nchunks, LANES), lambda bi, h: (bi, h, 0, 0))),
        scratch_shapes=[pltpu.VMEM((s, d), F32)] * 3 + [pltpu.VMEM((nchunks, CHUNK), F32)] * 2,
        compiler_params=_params("parallel", "parallel"),
        name="gdn_prep",
    )(proj, proj, proj, conv_w, conv_w, conv_w, ba_t, ba_t, alog_b, dtb_b)

    nc = b * heads * nchunks
    ncp = -(-nc // LANES) * LANES
    l_t = jnp.pad(lmat.reshape(nc, CHUNK * CHUNK).T, ((0, 0), (0, ncp - nc))).reshape(CHUNK, CHUNK, ncp)
    blk = pl.BlockSpec((CHUNK, CHUNK, LANES), lambda i: (0, 0, i))
    t_t = pl.pallas_call(
        _solve_kernel,
        out_shape=jax.ShapeDtypeStruct((CHUNK, CHUNK, ncp), F32),
        grid=(ncp // LANES,), in_specs=[blk], out_specs=blk,
        compiler_params=_params("parallel"),
        name="gdn_solve",
    )(l_t)
    t_mat = t_t.reshape(CHUNK * CHUNK, ncp)[:, :nc].T.reshape(b, heads, s, CHUNK)

    return pl.pallas_call(
        _gdn_scan_kernel,
        out_shape=jax.ShapeDtypeStruct((b, s, heads * d), BF16),
        grid=(b, heads),
        in_specs=[per_head(d), per_head(d), per_head(d), per_head(d), per_head(CHUNK), per_head(CHUNK),
                  pl.BlockSpec((None, None, nchunks, LANES), lambda bi, h: (bi, h, 0, 0)),
                  seq(3 * heads), pl.BlockSpec((1, d), lambda bi, h: (0, 0))],
        out_specs=pl.BlockSpec((None, s, d), lambda bi, h: (bi, 0, h)),
        compiler_params=_params("parallel", "parallel"),
        name="gdn_scan",
    )(qd, kt, vb, kbg, intra, t_mat, cd, proj, norm_g.reshape(1, d))


def kernel(x, mem, ffn_pre_g, ffn_post_g, mix_pre_g, mix_post_g, mem_g, ffn_w_gate, ffn_w_up, ffn_w_down,
           mem_w_kv, mix_w_out, sc_w_in, sc_conv_w, gdn_w_in, gdn_conv_w, gdn_a_log, gdn_dt_bias, gdn_norm_g):
    b, s, d = x.shape
    m = mem.shape[1]
    depth = ffn_pre_g.shape[0]
    xw = mem_w_kv.shape[-1] // 2
    mix = d - xw
    heads = mix // GDN_HEAD_DIM
    t = b * s

    h = x.reshape(t, d)
    mem2 = mem.reshape(b * m, d)
    hn = _norm_bf16(h, ffn_pre_g[0, 0])

    def ffn(h, hn, i, half, next_g):
        hidden = _swiglu_up(hn, ffn_w_gate, ffn_w_up, (i, half))
        y = _linear([hidden], ffn_w_down, (i, half), 0, d, F32, "ffn_down")
        return _residual(y, h, ffn_post_g[i, half], next_g, 0.5)

    for i in range(depth):
        j = i // N_MIXERS
        h, hn = ffn(h, hn, i, 0, mix_pre_g[i])

        mem_n = _norm_bf16(mem2, mem_g[i])
        kv = _linear([mem_n], mem_w_kv, (i,), 0, 2 * xw, F32, "mem_kv").reshape(b, m, 2 * xw)
        if i % N_MIXERS == 0:
            proj = _linear([hn], sc_w_in, (j,), 0, 3 * mix + xw, F32, "sc_in").reshape(b, s, 3 * mix + xw)
            y = _short_conv(proj, sc_conv_w[j], mix)
            xo = _cross_attention(proj, 3 * mix // xw, kv, xw)
        else:
            proj = _linear([hn], gdn_w_in, (j,), 0, 4 * mix, F32, "gdn_in").reshape(b, s, 4 * mix)
            w_tail = gdn_w_in[j][:, 4 * mix:]
            w_ba = jnp.pad(w_tail[:, :2 * heads], ((0, 0), (0, LANES - 2 * heads)))
            ba = _linear([hn], w_ba, (), 0, LANES, F32, "gdn_gates").reshape(b, s, LANES)
            xq = _linear([hn], w_tail[:, 2 * heads:], (), 0, xw, F32, "gdn_xq").reshape(b, s, xw)
            y = _gated_deltanet(proj, ba, gdn_conv_w[j], gdn_a_log[j], gdn_dt_bias[j], gdn_norm_g[j], heads)
            xo = _cross_attention(xq, 0, kv, xw)
        mixed = _linear([y.reshape(t, mix), xo.reshape(t, xw)], mix_w_out, (i,), 0, d, F32, "mix_out")
        h, hn = _residual(mixed, h, mix_post_g[i], ffn_pre_g[i, 1], 1.0)

        next_g = ffn_pre_g[i + 1, 0] if i + 1 < depth else None
        h, hn = ffn(h, hn, i, 1, next_g)
    return h.reshape(b, s, d)
```

```python
import functools

import jax
import jax.numpy as jnp
from jax import lax
from jax.experimental import pallas as pl
from jax.experimental.pallas import tpu as pltpu

CHUNK = 64
N_MIXERS = 2
X_HEADS = 4
GDN_HEAD_DIM = 128
EPS = 1e-6
LANES = 128
V7X_VMEM_BYTES = 64 * 2**20
VMEM_LIMIT_BYTES = V7X_VMEM_BYTES - 6 * 2**20

F32 = jnp.float32
BF16 = jnp.bfloat16
NT = (((1,), (1,)), ((), ()))
TN = (((0,), (0,)), ((), ()))


def _tile(n, prefs):
    for p in prefs:
        if n % p == 0:
            return p
    return n


def _params(*semantics):
    return pltpu.CompilerParams(dimension_semantics=semantics, vmem_limit_bytes=VMEM_LIMIT_BYTES)


def _rms(x, g):
    return x * lax.rsqrt(jnp.mean(x * x, axis=-1, keepdims=True) + EPS) * g


def _silu(x):
    return x * jax.nn.sigmoid(x)


def _causal_conv(x, w, prev=None):
    width = w.shape[0]
    row = lax.broadcasted_iota(jnp.int32, x.shape, 0)
    acc = None
    for j in range(width):
        s = width - 1 - j
        if s == 0:
            xs = x
        else:
            head = 0.0 if prev is None else pltpu.roll(prev, s, 0)
            xs = jnp.where(row >= s, pltpu.roll(x, s, 0), head)
        term = xs * w[j:j + 1, :]
        acc = term if acc is None else acc + term
    return acc


def _norm_kernel(x_ref, g_ref, o_ref):
    o_ref[...] = _rms(x_ref[...], g_ref[...]).astype(o_ref.dtype)


def _norm_bf16(x, g):
    t, d = x.shape
    tr = _tile(t, (256, 128, 64, 32, 16))
    return pl.pallas_call(
        _norm_kernel,
        out_shape=jax.ShapeDtypeStruct((t, d), BF16),
        grid=(t // tr,),
        in_specs=[pl.BlockSpec((tr, d), lambda i: (i, 0)), pl.BlockSpec((1, d), lambda i: (0, 0))],
        out_specs=pl.BlockSpec((tr, d), lambda i: (i, 0)),
        compiler_params=_params("parallel"),
        name="norm",
    )(x, g.reshape(1, d))


def _resid_kernel(y_ref, h_ref, pg_ref, ng_ref, h_out, hn_out, *, weight):
    hnew = h_ref[...] + weight * _rms(y_ref[...], pg_ref[...])
    h_out[...] = hnew
    hn_out[...] = _rms(hnew, ng_ref[...]).astype(hn_out.dtype)


def _resid_last_kernel(y_ref, h_ref, pg_ref, h_out, *, weight):
    h_out[...] = h_ref[...] + weight * _rms(y_ref[...], pg_ref[...])


def _residual(y, h, post_g, next_g, weight):
    t, d = h.shape
    tr = _tile(t, (256, 128, 64, 32, 16))
    row = pl.BlockSpec((tr, d), lambda i: (i, 0))
    vec = pl.BlockSpec((1, d), lambda i: (0, 0))
    if next_g is None:
        return pl.pallas_call(
            functools.partial(_resid_last_kernel, weight=weight),
            out_shape=jax.ShapeDtypeStruct((t, d), F32),
            grid=(t // tr,), in_specs=[row, row, vec], out_specs=row,
            compiler_params=_params("parallel"), name="residual_last",
        )(y, h, post_g.reshape(1, d)), None
    return pl.pallas_call(
        functools.partial(_resid_kernel, weight=weight),
        out_shape=(jax.ShapeDtypeStruct((t, d), F32), jax.ShapeDtypeStruct((t, d), BF16)),
        grid=(t // tr,), in_specs=[row, row, vec, vec], out_specs=(row, row),
        compiler_params=_params("parallel"), name="residual",
    )(y, h, post_g.reshape(1, d), next_g.reshape(1, d))


def _linear_kernel(*refs, n_x, w_transposed):
    x_refs, w_ref, o_ref = refs[:n_x], refs[n_x], refs[n_x + 1]
    acc, k0 = None, 0
    for x_ref in x_refs:
        kk = x_ref.shape[1]
        if w_transposed:
            part = lax.dot_general(x_ref[...], w_ref[:, k0:k0 + kk].astype(BF16), NT, preferred_element_type=F32)
        else:
            part = jnp.dot(x_ref[...], w_ref[k0:k0 + kk, :].astype(BF16), preferred_element_type=F32)
        acc = part if acc is None else acc + part
        k0 += kk
    o_ref[...] = acc.astype(o_ref.dtype)


def _linear(xs, w, lead, col0, ncols, out_dtype, name, w_transposed=False):
    t = xs[0].shape[0]
    k = sum(x.shape[1] for x in xs)
    assert w.shape[-1 if w_transposed else -2] == k
    tm = _tile(t, (1024, 512, 256, 128, 64, 32, 16))
    tn = _tile(ncols, (512, 256, 128))
    assert col0 % tn == 0
    cb = col0 // tn
    squeezed = (None,) * len(lead)
    if w_transposed:
        w_spec = pl.BlockSpec(squeezed + (tn, k), lambda i, j: tuple(lead) + (cb + j, 0))
    else:
        w_spec = pl.BlockSpec(squeezed + (k, tn), lambda i, j: tuple(lead) + (0, cb + j))
    x_specs = [pl.BlockSpec((tm, x.shape[1]), lambda i, j: (i, 0)) for x in xs]
    return pl.pallas_call(
        functools.partial(_linear_kernel, n_x=len(xs), w_transposed=w_transposed),
        out_shape=jax.ShapeDtypeStruct((t, ncols), out_dtype),
        grid=(t // tm, ncols // tn),
        in_specs=x_specs + [w_spec],
        out_specs=pl.BlockSpec((tm, tn), lambda i, j: (i, j)),
        compiler_params=_params("parallel", "arbitrary"),
        name=name,
    )(*xs, w)


def _swiglu_up_kernel(x_ref, wg_ref, wu_ref, o_ref):
    x = x_ref[...]
    gate = jnp.dot(x, wg_ref[...].astype(BF16), preferred_element_type=F32)
    up = jnp.dot(x, wu_ref[...].astype(BF16), preferred_element_type=F32)
    o_ref[...] = (_silu(gate) * up).astype(o_ref.dtype)


def _swiglu_up(x, w_gate, w_up, lead):
    t, k = x.shape
    f = w_gate.shape[-1]
    tm = _tile(t, (1024, 512, 256, 128, 64, 32, 16))
    tn = _tile(f, (256, 128))
    w_spec = pl.BlockSpec((None,) * len(lead) + (k, tn), lambda i, j: tuple(lead) + (0, j))
    return pl.pallas_call(
        _swiglu_up_kernel,
        out_shape=jax.ShapeDtypeStruct((t, f), BF16),
        grid=(t // tm, f // tn),
        in_specs=[pl.BlockSpec((tm, k), lambda i, j: (i, 0)), w_spec, w_spec],
        out_specs=pl.BlockSpec((tm, tn), lambda i, j: (i, j)),
        compiler_params=_params("parallel", "arbitrary"),
        name="swiglu_up",
    )(x, w_gate, w_up)


def _sconv_kernel(b_ref, c_ref, h_ref, w_ref, o_ref):
    u = c_ref[...] * h_ref[...]
    o_ref[...] = (b_ref[...] * _causal_conv(u, w_ref[...])).astype(o_ref.dtype)


def _short_conv(proj, conv_w, mix):
    b, s, _ = proj.shape
    tc = _tile(mix, (256, 128))
    nb = mix // tc
    blk = lambda off: pl.BlockSpec((None, s, tc), lambda bi, j: (bi, 0, off + j))
    return pl.pallas_call(
        _sconv_kernel,
        out_shape=jax.ShapeDtypeStruct((b, s, mix), BF16),
        grid=(b, nb),
        in_specs=[blk(0), blk(nb), blk(2 * nb), pl.BlockSpec((conv_w.shape[0], tc), lambda bi, j: (0, j))],
        out_specs=pl.BlockSpec((None, s, tc), lambda bi, j: (bi, 0, j)),
        compiler_params=_params("parallel", "parallel"),
        name="short_conv",
    )(proj, proj, proj, conv_w)


def _xattn_kernel(q_ref, k_ref, v_ref, o_ref):
    dh = q_ref.shape[-1] // X_HEADS
    scale = dh ** -0.5
    for hd in range(X_HEADS):
        sl = slice(hd * dh, (hd + 1) * dh)
        q = q_ref[:, sl].astype(BF16)
        k = k_ref[:, sl].astype(BF16)
        v = v_ref[:, sl].astype(BF16)
        s = lax.dot_general(q, k, NT, preferred_element_type=F32) * scale
        e = jnp.exp(s - jnp.max(s, axis=-1, keepdims=True))
        p = e / jnp.sum(e, axis=-1, keepdims=True)
        o_ref[:, sl] = jnp.dot(p.astype(BF16), v, preferred_element_type=F32).astype(o_ref.dtype)


def _cross_attention(q_arr, q_blk, kv, xw):
    b, s, _ = q_arr.shape
    m = kv.shape[1]
    tq = _tile(s, (512, 256, 128, 64))
    return pl.pallas_call(
        _xattn_kernel,
        out_shape=jax.ShapeDtypeStruct((b, s, xw), BF16),
        grid=(b, s // tq),
        in_specs=[pl.BlockSpec((None, tq, xw), lambda bi, i: (bi, i, q_blk)),
                  pl.BlockSpec((None, m, xw), lambda bi, i: (bi, 0, 0)),
                  pl.BlockSpec((None, m, xw), lambda bi, i: (bi, 0, 1))],
        out_specs=pl.BlockSpec((None, tq, xw), lambda bi, i: (bi, i, 0)),
        compiler_params=_params("parallel", "parallel"),
        name="cross_attention",
    )(q_arr, kv, kv)


def _gdn_prep_kernel(qkv_ref, cw_ref, ba_ref, alog_ref, dtb_ref,
                     qd_ref, kt_ref, vbk_ref, intra_ref, l_ref, cd_ref, prev_s, *, heads):
    d, c = GDN_HEAD_DIM, CHUNK
    mix = heads * d

    @pl.when(pl.program_id(1) == 0)
    def _():
        prev_s[...] = jnp.zeros_like(prev_s)

    cur = qkv_ref[...]
    x = _silu(_causal_conv(cur, cw_ref[...], prev_s[...]))
    prev_s[...] = cur

    ba = ba_ref[...]
    beta = jax.nn.sigmoid(ba)
    xg = ba + dtb_ref[...]
    softplus = jnp.maximum(xg, 0.0) + jnp.log1p(jnp.exp(-jnp.abs(xg)))
    g = -jnp.exp(alog_ref[...]) * softplus
    ii = lax.broadcasted_iota(jnp.int32, (c, c), 0)
    jj = lax.broadcasted_iota(jnp.int32, (c, c), 1)
    tri, strict = ii >= jj, ii > jj
    gc = jnp.dot(tri.astype(F32), g, precision=lax.Precision.HIGHEST, preferred_element_type=F32)
    gc_t = gc.T
    lane = lax.broadcasted_iota(jnp.int32, ba.shape, 1)

    def column(arr, idx):
        return jnp.sum(jnp.where(lane == idx, arr, 0.0), axis=1, keepdims=True)

    def l2n(v):
        return v * lax.rsqrt(jnp.sum(v * v, axis=-1, keepdims=True) + EPS)

    qd, kt, vbk, intra, lmat, cd = [], [], [], [], [], []
    for h in range(heads):
        qn = l2n(x[:, h * d:(h + 1) * d]) * (d ** -0.5)
        kn = l2n(x[:, mix + h * d:mix + (h + 1) * d])
        vn = x[:, 2 * mix + h * d:2 * mix + (h + 1) * d]
        g_col = column(gc, heads + h)
        b_col = column(beta, h)
        g_row = gc_t[heads + h:heads + h + 1, :]
        g_last = g_row[:, c - 1:c]
        decay = jnp.where(tri, jnp.exp(jnp.where(tri, g_col - g_row, 0.0)), 0.0)
        kb = kn * b_col
        lhs = jnp.concatenate([kb.astype(BF16), qn.astype(BF16)], axis=0)
        prod = lax.dot_general(lhs, kn.astype(BF16), NT, preferred_element_type=F32)
        lmat.append(jnp.where(strict, prod[:c] * decay, 0.0))
        intra.append(jnp.where(tri, prod[c:] * decay, 0.0).astype(BF16))
        eg = jnp.exp(g_col)
        qd.append((qn * eg).astype(BF16))
        kt.append((kn * jnp.exp(g_last - g_col)).astype(BF16))
        vbk.append((vn * b_col).astype(BF16))
        vbk.append((kb * eg).astype(BF16))
        cd.append(jnp.broadcast_to(jnp.exp(g_last), (1, d)))
    qd_ref[...] = jnp.concatenate(qd, axis=1)
    kt_ref[...] = jnp.concatenate(kt, axis=1)
    vbk_ref[...] = jnp.concatenate(vbk, axis=1)
    intra_ref[...] = jnp.concatenate(intra, axis=1)
    l_ref[...] = jnp.concatenate(lmat, axis=1)
    cd_ref[...] = jnp.concatenate(cd, axis=1)


def _solve_kernel(l_ref, t_ref):
    c = l_ref.shape[0]
    col_id = lax.broadcasted_iota(jnp.int32, (c, t_ref.shape[2]), 0)
    t_ref[0] = jnp.where(col_id == 0, 1.0, 0.0)

    def row(r, carry):
        def term(m, acc):
            return acc - l_ref[r, pl.ds(m, 1), :] * t_ref[m]
        t_ref[r] = lax.fori_loop(0, r, term, jnp.where(col_id == r, 1.0, 0.0))
        return carry

    lax.fori_loop(1, c, row, 0)


def _gdn_scan_kernel(qd_ref, kt_ref, vbk_ref, intra_ref, t_ref, cd_ref, z_ref, ng_ref, y_ref, st, *, heads):
    d, c = GDN_HEAD_DIM, CHUNK

    @pl.when(pl.program_id(1) == 0)
    def _():
        st[...] = jnp.zeros_like(st)

    ng = ng_ref[...]
    hs = range(heads)
    sl = [slice(h * d, (h + 1) * d) for h in hs]
    sc = [slice(h * c, (h + 1) * c) for h in hs]
    uw = [jnp.dot(t_ref[:, sc[h]].astype(BF16), vbk_ref[:, 2 * h * d:2 * (h + 1) * d],
                  preferred_element_type=F32) for h in hs]
    s = [st[h] for h in hs]
    ws_qs = [jnp.dot(jnp.concatenate([uw[h][:, d:].astype(BF16), qd_ref[:, sl[h]]], axis=0), s[h].astype(BF16),
                     preferred_element_type=F32) for h in hs]
    v16 = [(uw[h][:, :d] - ws_qs[h][:c]).astype(BF16) for h in hs]
    o = [ws_qs[h][c:] + jnp.dot(intra_ref[:, sc[h]], v16[h], preferred_element_type=F32) for h in hs]
    st[...] = jnp.stack([s[h] * cd_ref[:, sl[h]] + lax.dot_general(kt_ref[:, sl[h]], v16[h], TN,
                                                                     preferred_element_type=F32) for h in hs])
    o = [o[h] * lax.rsqrt(jnp.mean(o[h] * o[h], axis=-1, keepdims=True) + EPS) * ng for h in hs]
    y_ref[...] = (jnp.concatenate(o, axis=1) * _silu(z_ref[...])).astype(y_ref.dtype)


def _gated_deltanet(proj, ba, conv_w, a_log, dt_bias, norm_g, heads):
    b, s, _ = proj.shape
    d, c = GDN_HEAD_DIM, CHUNK
    mix = heads * d
    nchunks = s // c
    alog_l = jnp.pad(a_log, (heads, LANES - 2 * heads)).reshape(1, LANES)
    dtb_l = jnp.pad(dt_bias, (heads, LANES - 2 * heads)).reshape(1, LANES)

    tok = lambda width, blk=0: pl.BlockSpec((None, c, width), lambda bi, n: (bi, n, blk))
    per_chunk = pl.BlockSpec((None, None, 1, mix), lambda bi, n: (bi, n, 0, 0))
    whole = lambda arr: pl.BlockSpec(arr.shape, lambda bi, n: (0,) * arr.ndim)
    ts = lambda width, dt: jax.ShapeDtypeStruct((b, s, width), dt)

    qd, kt, vbk, intra, lmat, cd = pl.pallas_call(
        functools.partial(_gdn_prep_kernel, heads=heads),
        out_shape=(ts(mix, BF16), ts(mix, BF16), ts(2 * mix, BF16), ts(heads * c, BF16), ts(heads * c, F32),
                   jax.ShapeDtypeStruct((b, nchunks, 1, mix), F32)),
        grid=(b, nchunks),
        in_specs=[tok(3 * mix), whole(conv_w), tok(LANES), whole(alog_l), whole(dtb_l)],
        out_specs=(tok(mix), tok(mix), tok(2 * mix), tok(heads * c), tok(heads * c), per_chunk),
        scratch_shapes=[pltpu.VMEM((c, 3 * mix), F32)],
        compiler_params=_params("parallel", "arbitrary"),
        name="gdn_prep",
    )(proj, conv_w, ba, alog_l, dtb_l)

    nc = b * nchunks * heads
    ncp = -(-nc // LANES) * LANES
    l_t = lmat.reshape(b, nchunks, c, heads, c).transpose(2, 4, 0, 1, 3).reshape(c, c, nc)
    l_t = jnp.pad(l_t, ((0, 0), (0, 0), (0, ncp - nc)))
    blk = pl.BlockSpec((c, c, LANES), lambda i: (0, 0, i))
    t_t = pl.pallas_call(
        _solve_kernel,
        out_shape=jax.ShapeDtypeStruct((c, c, ncp), F32),
        grid=(ncp // LANES,), in_specs=[blk], out_specs=blk,
        compiler_params=_params("parallel"),
        name="gdn_solve",
    )(l_t)
    t_mat = t_t[:, :, :nc].reshape(c, c, b, nchunks, heads).transpose(2, 3, 0, 4, 1).reshape(b, s, heads * c)

    return pl.pallas_call(
        functools.partial(_gdn_scan_kernel, heads=heads),
        out_shape=ts(mix, BF16),
        grid=(b, nchunks),
        in_specs=[tok(mix), tok(mix), tok(2 * mix), tok(heads * c), tok(heads * c), per_chunk,
                  tok(mix, 3), pl.BlockSpec((1, d), lambda bi, n: (0, 0))],
        out_specs=tok(mix),
        scratch_shapes=[pltpu.VMEM((heads, d, d), F32)],
        compiler_params=_params("parallel", "arbitrary"),
        name="gdn_scan",
    )(qd, kt, vbk, intra, t_mat, cd, proj, norm_g.reshape(1, d))


def kernel(x, mem, ffn_pre_g, ffn_post_g, mix_pre_g, mix_post_g, mem_g, ffn_w_gate, ffn_w_up, ffn_w_down,
           mem_w_kv, mix_w_out, sc_w_in, sc_conv_w, gdn_w_in, gdn_conv_w, gdn_a_log, gdn_dt_bias, gdn_norm_g):
    b, s, d = x.shape
    m = mem.shape[1]
    depth = ffn_pre_g.shape[0]
    xw = mem_w_kv.shape[-1] // 2
    mix = d - xw
    heads = mix // GDN_HEAD_DIM
    t = b * s
    gdn_w_in_t = jnp.swapaxes(gdn_w_in, 1, 2)

    h = x.reshape(t, d)
    mem2 = mem.reshape(b * m, d)
    hn = _norm_bf16(h, ffn_pre_g[0, 0])

    def ffn(h, hn, i, half, next_g):
        hidden = _swiglu_up(hn, ffn_w_gate, ffn_w_up, (i, half))
        y = _linear([hidden], ffn_w_down, (i, half), 0, d, F32, "ffn_down")
        return _residual(y, h, ffn_post_g[i, half], next_g, 0.5)

    for i in range(depth):
        j = i // N_MIXERS
        h, hn = ffn(h, hn, i, 0, mix_pre_g[i])

        mem_n = _norm_bf16(mem2, mem_g[i])
        kv = _linear([mem_n], mem_w_kv, (i,), 0, 2 * xw, F32, "mem_kv").reshape(b, m, 2 * xw)
        if i % N_MIXERS == 0:
            proj = _linear([hn], sc_w_in, (j,), 0, 3 * mix + xw, F32, "sc_in").reshape(b, s, 3 * mix + xw)
            y = _short_conv(proj, sc_conv_w[j], mix)
            xo = _cross_attention(proj, 3 * mix // xw, kv, xw)
        else:
            proj = _linear([hn], gdn_w_in_t, (j,), 0, 4 * mix, F32, "gdn_in", w_transposed=True)
            w_tail = lax.slice(gdn_w_in_t, (j, 4 * mix, 0), (j + 1, gdn_w_in_t.shape[1], d))[0]
            w_ba = jnp.pad(w_tail[:2 * heads], ((0, LANES - 2 * heads), (0, 0)))
            ba = _linear([hn], w_ba, (), 0, LANES, F32, "gdn_gates", w_transposed=True)
            xq = _linear([hn], w_tail[2 * heads:], (), 0, xw, F32, "gdn_xq", w_transposed=True)
            y = _gated_deltanet(proj.reshape(b, s, 4 * mix), ba.reshape(b, s, LANES), gdn_conv_w[j],
                                gdn_a_log[j], gdn_dt_bias[j], gdn_norm_g[j], heads)
            xo = _cross_attention(xq.reshape(b, s, xw), 0, kv, xw)
        mixed = _linear([y.reshape(t, mix), xo.reshape(t, xw)], mix_w_out, (i,), 0, d, F32, "mix_out")
        h, hn = _residual(mixed, h, mix_post_g[i], ffn_pre_g[i, 1], 1.0)

        next_g = ffn_pre_g[i + 1, 0] if i + 1 < depth else None
        h, hn = ffn(h, hn, i, 1, next_g)
    return h.reshape(b, s, d)
```

```python
import functools

import jax
import jax.numpy as jnp
from jax import lax
from jax.experimental import pallas as pl
from jax.experimental.pallas import tpu as pltpu

CHUNK = 64
N_MIXERS = 2
X_HEADS = 4
GDN_HEAD_DIM = 128
EPS = 1e-6
LANES = 128
SUBLANES = 8
V7X_VMEM_BYTES = 64 * 2**20
VMEM_LIMIT_BYTES = V7X_VMEM_BYTES - 6 * 2**20

F32 = jnp.float32
BF16 = jnp.bfloat16
NT = (((1,), (1,)), ((), ()))
TN = (((0,), (0,)), ((), ()))


def _tile(n, prefs):
    for p in prefs:
        if n % p == 0:
            return p
    return n


def _params(*semantics):
    return pltpu.CompilerParams(dimension_semantics=semantics, vmem_limit_bytes=VMEM_LIMIT_BYTES)


def _rms(x, g):
    return x * lax.rsqrt(jnp.mean(x * x, axis=-1, keepdims=True) + EPS) * g


def _silu(x):
    return x * jax.nn.sigmoid(x)


def _causal_conv(x, w, tail=None):
    width = w.shape[0]
    assert width - 1 <= SUBLANES
    row = lax.broadcasted_iota(jnp.int32, (SUBLANES, x.shape[1]), 0)
    acc = None
    for j in range(width):
        s = width - 1 - j
        if s == 0:
            xs = x
        else:
            rolled = pltpu.roll(x, s, 0)
            fill = 0.0 if tail is None else pltpu.roll(tail, s, 0)
            xs = jnp.concatenate([jnp.where(row >= s, rolled[:SUBLANES], fill), rolled[SUBLANES:]], axis=0)
        term = xs * w[j:j + 1, :]
        acc = term if acc is None else acc + term
    return acc


def _norm_kernel(x_ref, g_ref, o_ref):
    o_ref[...] = _rms(x_ref[...], g_ref[...]).astype(o_ref.dtype)


def _norm_bf16(x, g):
    t, d = x.shape
    tr = _tile(t, (256, 128, 64, 32, 16))
    return pl.pallas_call(
        _norm_kernel,
        out_shape=jax.ShapeDtypeStruct((t, d), BF16),
        grid=(t // tr,),
        in_specs=[pl.BlockSpec((tr, d), lambda i: (i, 0)), pl.BlockSpec((1, d), lambda i: (0, 0))],
        out_specs=pl.BlockSpec((tr, d), lambda i: (i, 0)),
        compiler_params=_params("parallel"),
        name="norm",
    )(x, g.reshape(1, d))


def _resid_kernel(y_ref, h_ref, pg_ref, ng_ref, h_out, hn_out, *, weight):
    hnew = h_ref[...] + weight * _rms(y_ref[...], pg_ref[...])
    h_out[...] = hnew
    hn_out[...] = _rms(hnew, ng_ref[...]).astype(hn_out.dtype)


def _resid_last_kernel(y_ref, h_ref, pg_ref, h_out, *, weight):
    h_out[...] = h_ref[...] + weight * _rms(y_ref[...], pg_ref[...])


def _residual(y, h, post_g, next_g, weight):
    t, d = h.shape
    tr = _tile(t, (256, 128, 64, 32, 16))
    row = pl.BlockSpec((tr, d), lambda i: (i, 0))
    vec = pl.BlockSpec((1, d), lambda i: (0, 0))
    if next_g is None:
        return pl.pallas_call(
            functools.partial(_resid_last_kernel, weight=weight),
            out_shape=jax.ShapeDtypeStruct((t, d), F32),
            grid=(t // tr,), in_specs=[row, row, vec], out_specs=row,
            compiler_params=_params("parallel"), name="residual_last",
        )(y, h, post_g.reshape(1, d)), None
    return pl.pallas_call(
        functools.partial(_resid_kernel, weight=weight),
        out_shape=(jax.ShapeDtypeStruct((t, d), F32), jax.ShapeDtypeStruct((t, d), BF16)),
        grid=(t // tr,), in_specs=[row, row, vec, vec], out_specs=(row, row),
        compiler_params=_params("parallel"), name="residual",
    )(y, h, post_g.reshape(1, d), next_g.reshape(1, d))


def _linear_kernel(*refs, n_x, w_transposed):
    x_refs, w_ref, o_ref = refs[:n_x], refs[n_x], refs[n_x + 1]
    acc, k0 = None, 0
    for x_ref in x_refs:
        kk = x_ref.shape[1]
        if w_transposed:
            part = lax.dot_general(x_ref[...], w_ref[:, k0:k0 + kk].astype(BF16), NT, preferred_element_type=F32)
        else:
            part = jnp.dot(x_ref[...], w_ref[k0:k0 + kk, :].astype(BF16), preferred_element_type=F32)
        acc = part if acc is None else acc + part
        k0 += kk
    o_ref[...] = acc.astype(o_ref.dtype)


def _linear(xs, w, lead, col0, ncols, out_dtype, name, w_transposed=False):
    t = xs[0].shape[0]
    k = sum(x.shape[1] for x in xs)
    assert w.shape[-1 if w_transposed else -2] == k
    tm = _tile(t, (1024, 512, 256, 128, 64, 32, 16))
    tn = _tile(ncols, (512, 256, 128))
    assert col0 % tn == 0
    cb = col0 // tn
    squeezed = (None,) * len(lead)
    if w_transposed:
        w_spec = pl.BlockSpec(squeezed + (tn, k), lambda i, j: tuple(lead) + (cb + j, 0))
    else:
        w_spec = pl.BlockSpec(squeezed + (k, tn), lambda i, j: tuple(lead) + (0, cb + j))
    x_specs = [pl.BlockSpec((tm, x.shape[1]), lambda i, j: (i, 0)) for x in xs]
    return pl.pallas_call(
        functools.partial(_linear_kernel, n_x=len(xs), w_transposed=w_transposed),
        out_shape=jax.ShapeDtypeStruct((t, ncols), out_dtype),
        grid=(t // tm, ncols // tn),
        in_specs=x_specs + [w_spec],
        out_specs=pl.BlockSpec((tm, tn), lambda i, j: (i, j)),
        compiler_params=_params("parallel", "arbitrary"),
        name=name,
    )(*xs, w)


def _swiglu_up_kernel(x_ref, wg_ref, wu_ref, o_ref):
    x = x_ref[...]
    gate = jnp.dot(x, wg_ref[...].astype(BF16), preferred_element_type=F32)
    up = jnp.dot(x, wu_ref[...].astype(BF16), preferred_element_type=F32)
    o_ref[...] = (_silu(gate) * up).astype(o_ref.dtype)


def _swiglu_up(x, w_gate, w_up, lead):
    t, k = x.shape
    f = w_gate.shape[-1]
    tm = _tile(t, (1024, 512, 256, 128, 64, 32, 16))
    tn = _tile(f, (256, 128))
    w_spec = pl.BlockSpec((None,) * len(lead) + (k, tn), lambda i, j: tuple(lead) + (0, j))
    return pl.pallas_call(
        _swiglu_up_kernel,
        out_shape=jax.ShapeDtypeStruct((t, f), BF16),
        grid=(t // tm, f // tn),
        in_specs=[pl.BlockSpec((tm, k), lambda i, j: (i, 0)), w_spec, w_spec],
        out_specs=pl.BlockSpec((tm, tn), lambda i, j: (i, j)),
        compiler_params=_params("parallel", "arbitrary"),
        name="swiglu_up",
    )(x, w_gate, w_up)


def _sconv_kernel(b_ref, c_ref, h_ref, w_ref, o_ref):
    u = c_ref[...] * h_ref[...]
    o_ref[...] = (b_ref[...] * _causal_conv(u, w_ref[...])).astype(o_ref.dtype)


def _short_conv(proj, conv_w, mix):
    b, s, _ = proj.shape
    tc = _tile(mix, (256, 128))
    nb = mix // tc
    blk = lambda off: pl.BlockSpec((None, s, tc), lambda bi, j: (bi, 0, off + j))
    return pl.pallas_call(
        _sconv_kernel,
        out_shape=jax.ShapeDtypeStruct((b, s, mix), BF16),
        grid=(b, nb),
        in_specs=[blk(0), blk(nb), blk(2 * nb), pl.BlockSpec((conv_w.shape[0], tc), lambda bi, j: (0, j))],
        out_specs=pl.BlockSpec((None, s, tc), lambda bi, j: (bi, 0, j)),
        compiler_params=_params("parallel", "parallel"),
        name="short_conv",
    )(proj, proj, proj, conv_w)


def _xattn_kernel(q_ref, k_ref, v_ref, o_ref):
    dh = q_ref.shape[-1] // X_HEADS
    scale = dh ** -0.5
    for hd in range(X_HEADS):
        sl = slice(hd * dh, (hd + 1) * dh)
        q = q_ref[:, sl].astype(BF16)
        k = k_ref[:, sl].astype(BF16)
        v = v_ref[:, sl].astype(BF16)
        s = lax.dot_general(q, k, NT, preferred_element_type=F32) * scale
        e = jnp.exp(s - jnp.max(s, axis=-1, keepdims=True))
        p = e / jnp.sum(e, axis=-1, keepdims=True)
        o_ref[:, sl] = jnp.dot(p.astype(BF16), v, preferred_element_type=F32).astype(o_ref.dtype)


def _cross_attention(q_arr, q_blk, kv, xw):
    b, s, _ = q_arr.shape
    m = kv.shape[1]
    tq = _tile(s, (512, 256, 128, 64))
    return pl.pallas_call(
        _xattn_kernel,
        out_shape=jax.ShapeDtypeStruct((b, s, xw), BF16),
        grid=(b, s // tq),
        in_specs=[pl.BlockSpec((None, tq, xw), lambda bi, i: (bi, i, q_blk)),
                  pl.BlockSpec((None, m, xw), lambda bi, i: (bi, 0, 0)),
                  pl.BlockSpec((None, m, xw), lambda bi, i: (bi, 0, 1))],
        out_specs=pl.BlockSpec((None, tq, xw), lambda bi, i: (bi, i, 0)),
        compiler_params=_params("parallel", "parallel"),
        name="cross_attention",
    )(q_arr, kv, kv)


def _gdn_prep_kernel(qkv_ref, cw_ref, ba_ref, alog_ref, dtb_ref,
                     qd_ref, kt_ref, vbk_ref, intra_ref, l_ref, cd_ref, prev_s, *, heads):
    d, c = GDN_HEAD_DIM, CHUNK
    mix = heads * d

    @pl.when(pl.program_id(1) == 0)
    def _():
        prev_s[...] = jnp.zeros_like(prev_s)

    cur = qkv_ref[...]
    x = _silu(_causal_conv(cur, cw_ref[...], prev_s[...]))
    prev_s[...] = cur[c - SUBLANES:, :]

    ba = ba_ref[...]
    beta = jax.nn.sigmoid(ba)
    xg = ba + dtb_ref[...]
    softplus = jnp.maximum(xg, 0.0) + jnp.log1p(jnp.exp(-jnp.abs(xg)))
    g = -jnp.exp(alog_ref[...]) * softplus
    ii = lax.broadcasted_iota(jnp.int32, (c, c), 0)
    jj = lax.broadcasted_iota(jnp.int32, (c, c), 1)
    tri, strict = ii >= jj, ii > jj
    gc = jnp.dot(tri.astype(F32), g, precision=lax.Precision.HIGHEST, preferred_element_type=F32)
    gc_t = gc.T
    lane = lax.broadcasted_iota(jnp.int32, ba.shape, 1)

    def column(arr, idx):
        return jnp.sum(jnp.where(lane == idx, arr, 0.0), axis=1, keepdims=True)

    def l2n(v):
        return v * lax.rsqrt(jnp.sum(v * v, axis=-1, keepdims=True) + EPS)

    qd, kt, vbk, intra, lmat, cd = [], [], [], [], [], []
    for h in range(heads):
        qn = l2n(x[:, h * d:(h + 1) * d]) * (d ** -0.5)
        kn = l2n(x[:, mix + h * d:mix + (h + 1) * d])
        vn = x[:, 2 * mix + h * d:2 * mix + (h + 1) * d]
        g_col = column(gc, heads + h)
        b_col = column(beta, h)
        g_row = gc_t[heads + h:heads + h + 1, :]
        g_last = g_row[:, c - 1:c]
        decay = jnp.where(tri, jnp.exp(jnp.where(tri, g_col - g_row, 0.0)), 0.0)
        kb = kn * b_col
        lhs = jnp.concatenate([kb.astype(BF16), qn.astype(BF16)], axis=0)
        prod = lax.dot_general(lhs, kn.astype(BF16), NT, preferred_element_type=F32)
        lmat.append(jnp.where(strict, prod[:c] * decay, 0.0))
        intra.append(jnp.where(tri, prod[c:] * decay, 0.0).astype(BF16))
        eg = jnp.exp(g_col)
        qd.append((qn * eg).astype(BF16))
        kt.append((kn * jnp.exp(g_last - g_col)).astype(BF16))
        vbk.append((vn * b_col).astype(BF16))
        vbk.append((kb * eg).astype(BF16))
        cd.append(jnp.broadcast_to(jnp.exp(g_last), (1, d)))
    qd_ref[...] = jnp.concatenate(qd, axis=1)
    kt_ref[...] = jnp.concatenate(kt, axis=1)
    vbk_ref[...] = jnp.concatenate(vbk, axis=1)
    intra_ref[...] = jnp.concatenate(intra, axis=1)
    l_ref[...] = jnp.concatenate(lmat, axis=1)
    cd_ref[...] = jnp.concatenate(cd, axis=1)


def _solve_kernel(l_ref, t_ref):
    c, lanes = l_ref.shape[0], t_ref.shape[2]
    t_ref[...] = jnp.zeros_like(t_ref)
    for rb in range(c // SUBLANES):
        hi = SUBLANES * (rb + 1)
        col_id = lax.broadcasted_iota(jnp.int32, (hi, lanes), 0)

        def row(rr, carry, rb=rb, hi=hi, col_id=col_id):
            r = rb * SUBLANES + rr
            acc = jnp.where(col_id == r, 1.0, 0.0)
            for mb in range(rb + 1):
                w = SUBLANES * (mb + 1)
                terms = [l_ref[r, m:m + 1, :] * t_ref[m, 0:w, :] for m in range(mb * SUBLANES, w)]
                while len(terms) > 1:
                    terms = [a + b for a, b in zip(terms[::2], terms[1::2])]
                part = acc[:w] - terms[0]
                acc = part if w == hi else jnp.concatenate([part, acc[w:]], axis=0)
            t_ref[r, 0:hi, :] = acc
            return carry

        lax.fori_loop(0, SUBLANES, row, 0)


def _gdn_scan_kernel(qd_ref, kt_ref, vbk_ref, intra_ref, t_ref, cd_ref, z_ref, ng_ref, y_ref, st, *, heads):
    d, c = GDN_HEAD_DIM, CHUNK

    @pl.when(pl.program_id(1) == 0)
    def _():
        st[...] = jnp.zeros_like(st)

    ng = ng_ref[...]
    hs = range(heads)
    sl = [slice(h * d, (h + 1) * d) for h in hs]
    sc = [slice(h * c, (h + 1) * c) for h in hs]
    uw = [jnp.dot(t_ref[:, sc[h]].astype(BF16), vbk_ref[:, 2 * h * d:2 * (h + 1) * d],
                  preferred_element_type=F32) for h in hs]
    s = [st[h] for h in hs]
    ws_qs = [jnp.dot(jnp.concatenate([uw[h][:, d:].astype(BF16), qd_ref[:, sl[h]]], axis=0), s[h].astype(BF16),
                     preferred_element_type=F32) for h in hs]
    v16 = [(uw[h][:, :d] - ws_qs[h][:c]).astype(BF16) for h in hs]
    o = [ws_qs[h][c:] + jnp.dot(intra_ref[:, sc[h]], v16[h], preferred_element_type=F32) for h in hs]
    st[...] = jnp.stack([s[h] * cd_ref[:, sl[h]] + lax.dot_general(kt_ref[:, sl[h]], v16[h], TN,
                                                                     preferred_element_type=F32) for h in hs])
    o = [o[h] * lax.rsqrt(jnp.mean(o[h] * o[h], axis=-1, keepdims=True) + EPS) * ng for h in hs]
    y_ref[...] = (jnp.concatenate(o, axis=1) * _silu(z_ref[...])).astype(y_ref.dtype)


def _gated_deltanet(proj, ba, conv_w, a_log, dt_bias, norm_g, heads):
    b, s, _ = proj.shape
    d, c = GDN_HEAD_DIM, CHUNK
    mix = heads * d
    nchunks = s // c
    alog_l = jnp.pad(a_log, (heads, LANES - 2 * heads)).reshape(1, LANES)
    dtb_l = jnp.pad(dt_bias, (heads, LANES - 2 * heads)).reshape(1, LANES)

    tok = lambda width, blk=0: pl.BlockSpec((None, c, width), lambda bi, n: (bi, n, blk))
    per_chunk = pl.BlockSpec((None, None, 1, mix), lambda bi, n: (bi, n, 0, 0))
    whole = lambda arr: pl.BlockSpec(arr.shape, lambda bi, n: (0,) * arr.ndim)
    ts = lambda width, dt: jax.ShapeDtypeStruct((b, s, width), dt)

    qd, kt, vbk, intra, lmat, cd = pl.pallas_call(
        functools.partial(_gdn_prep_kernel, heads=heads),
        out_shape=(ts(mix, BF16), ts(mix, BF16), ts(2 * mix, BF16), ts(heads * c, BF16), ts(heads * c, F32),
                   jax.ShapeDtypeStruct((b, nchunks, 1, mix), F32)),
        grid=(b, nchunks),
        in_specs=[tok(3 * mix), whole(conv_w), tok(LANES), whole(alog_l), whole(dtb_l)],
        out_specs=(tok(mix), tok(mix), tok(2 * mix), tok(heads * c), tok(heads * c), per_chunk),
        scratch_shapes=[pltpu.VMEM((SUBLANES, 3 * mix), F32)],
        compiler_params=_params("parallel", "arbitrary"),
        name="gdn_prep",
    )(proj, conv_w, ba, alog_l, dtb_l)

    nb = b * nchunks
    nbp = -(-nb // LANES) * LANES
    l_t = jnp.pad(lmat.reshape(nb, c * heads * c).T, ((0, 0), (0, nbp - nb))).reshape(c, heads * c, nbp)
    blk = pl.BlockSpec((c, c, LANES), lambda h, i: (0, h, i))
    t_t = pl.pallas_call(
        _solve_kernel,
        out_shape=jax.ShapeDtypeStruct((c, heads * c, nbp), F32),
        grid=(heads, nbp // LANES), in_specs=[blk], out_specs=blk,
        compiler_params=_params("parallel", "parallel"),
        name="gdn_solve",
    )(l_t)
    t_mat = t_t.reshape(c * heads * c, nbp)[:, :nb].T.reshape(b, s, heads * c)

    return pl.pallas_call(
        functools.partial(_gdn_scan_kernel, heads=heads),
        out_shape=ts(mix, BF16),
        grid=(b, nchunks),
        in_specs=[tok(mix), tok(mix), tok(2 * mix), tok(heads * c), tok(heads * c), per_chunk,
                  tok(mix, 3), pl.BlockSpec((1, d), lambda bi, n: (0, 0))],
        out_specs=tok(mix),
        scratch_shapes=[pltpu.VMEM((heads, d, d), F32)],
        compiler_params=_params("parallel", "arbitrary"),
        name="gdn_scan",
    )(qd, kt, vbk, intra, t_mat, cd, proj, norm_g.reshape(1, d))


def kernel(x, mem, ffn_pre_g, ffn_post_g, mix_pre_g, mix_post_g, mem_g, ffn_w_gate, ffn_w_up, ffn_w_down,
           mem_w_kv, mix_w_out, sc_w_in, sc_conv_w, gdn_w_in, gdn_conv_w, gdn_a_log, gdn_dt_bias, gdn_norm_g):
    b, s, d = x.shape
    m = mem.shape[1]
    depth = ffn_pre_g.shape[0]
    xw = mem_w_kv.shape[-1] // 2
    mix = d - xw
    heads = mix // GDN_HEAD_DIM
    t = b * s
    gdn_w_in_t = jnp.swapaxes(gdn_w_in, 1, 2)

    h = x.reshape(t, d)
    mem2 = mem.reshape(b * m, d)
    hn = _norm_bf16(h, ffn_pre_g[0, 0])

    def ffn(h, hn, i, half, next_g):
        hidden = _swiglu_up(hn, ffn_w_gate, ffn_w_up, (i, half))
        y = _linear([hidden], ffn_w_down, (i, half), 0, d, F32, "ffn_down")
        return _residual(y, h, ffn_post_g[i, half], next_g, 0.5)

    for i in range(depth):
        j = i // N_MIXERS
        h, hn = ffn(h, hn, i, 0, mix_pre_g[i])

        mem_n = _norm_bf16(mem2, mem_g[i])
        kv = _linear([mem_n], mem_w_kv, (i,), 0, 2 * xw, F32, "mem_kv").reshape(b, m, 2 * xw)
        if i % N_MIXERS == 0:
            proj = _linear([hn], sc_w_in, (j,), 0, 3 * mix + xw, F32, "sc_in").reshape(b, s, 3 * mix + xw)
            y = _short_conv(proj, sc_conv_w[j], mix)
            xo = _cross_attention(proj, 3 * mix // xw, kv, xw)
        else:
            proj = _linear([hn], gdn_w_in_t, (j,), 0, 4 * mix, F32, "gdn_in", w_transposed=True)
            w_tail = lax.slice(gdn_w_in_t, (j, 4 * mix, 0), (j + 1, gdn_w_in_t.shape[1], d))[0]
            w_ba = jnp.pad(w_tail[:2 * heads], ((0, LANES - 2 * heads), (0, 0)))
            ba = _linear([hn], w_ba, (), 0, LANES, F32, "gdn_gates", w_transposed=True)
            xq = _linear([hn], w_tail[2 * heads:], (), 0, xw, F32, "gdn_xq", w_transposed=True)
            y = _gated_deltanet(proj.reshape(b, s, 4 * mix), ba.reshape(b, s, LANES), gdn_conv_w[j],
                                gdn_a_log[j], gdn_dt_bias[j], gdn_norm_g[j], heads)
            xo = _cross_attention(xq.reshape(b, s, xw), 0, kv, xw)
        mixed = _linear([y.reshape(t, mix), xo.reshape(t, xw)], mix_w_out, (i,), 0, d, F32, "mix_out")
        h, hn = _residual(mixed, h, mix_post_g[i], ffn_pre_g[i, 1], 1.0)

        next_g = ffn_pre_g[i + 1, 0] if i + 1 < depth else None
        h, hn = ffn(h, hn, i, 1, next_g)
    return h.reshape(b, s, d)
```

```python
import functools

import jax
import jax.numpy as jnp
from jax import lax
from jax.experimental import pallas as pl
from jax.experimental.pallas import tpu as pltpu

CHUNK = 64
N_MIXERS = 2
X_HEADS = 4
GDN_HEAD_DIM = 128
EPS = 1e-6
LANES = 128
SUBLANES = 8
V7X_VMEM_BYTES = 64 * 2**20
VMEM_LIMIT_BYTES = V7X_VMEM_BYTES - 6 * 2**20

F32 = jnp.float32
BF16 = jnp.bfloat16
NT = (((1,), (1,)), ((), ()))
TN = (((0,), (0,)), ((), ()))


def _tile(n, prefs):
    for p in prefs:
        if n % p == 0:
            return p
    return n


def _params(*semantics):
    return pltpu.CompilerParams(dimension_semantics=semantics, vmem_limit_bytes=VMEM_LIMIT_BYTES)


def _rms(x, g):
    return x * lax.rsqrt(jnp.mean(x * x, axis=-1, keepdims=True) + EPS) * g


def _silu(x):
    return x * jax.nn.sigmoid(x)


def _causal_conv(x, w, tail=None):
    width = w.shape[0]
    assert width - 1 <= SUBLANES
    row = lax.broadcasted_iota(jnp.int32, (SUBLANES, x.shape[1]), 0)
    acc = None
    for j in range(width):
        s = width - 1 - j
        if s == 0:
            xs = x
        else:
            rolled = pltpu.roll(x, s, 0)
            fill = 0.0 if tail is None else pltpu.roll(tail, s, 0)
            xs = jnp.concatenate([jnp.where(row >= s, rolled[:SUBLANES], fill), rolled[SUBLANES:]], axis=0)
        term = xs * w[j:j + 1, :]
        acc = term if acc is None else acc + term
    return acc


def _norm_kernel(x_ref, g_ref, o_ref):
    o_ref[...] = _rms(x_ref[...], g_ref[...]).astype(o_ref.dtype)


def _norm_bf16(x, g):
    t, d = x.shape
    tr = _tile(t, (256, 128, 64, 32, 16))
    return pl.pallas_call(
        _norm_kernel,
        out_shape=jax.ShapeDtypeStruct((t, d), BF16),
        grid=(t // tr,),
        in_specs=[pl.BlockSpec((tr, d), lambda i: (i, 0)), pl.BlockSpec((1, d), lambda i: (0, 0))],
        out_specs=pl.BlockSpec((tr, d), lambda i: (i, 0)),
        compiler_params=_params("parallel"),
        name="norm",
    )(x, g.reshape(1, d))


def _resid_kernel(y_ref, h_ref, pg_ref, h_out, *, weight):
    h_out[...] = h_ref[...] + weight * _rms(y_ref[...], pg_ref[...])


def _residual(y, h, post_g, weight):
    t, d = h.shape
    tr = _tile(t, (256, 128, 64, 32, 16))
    row = pl.BlockSpec((tr, d), lambda i: (i, 0))
    vec = pl.BlockSpec((1, d), lambda i: (0, 0))
    return pl.pallas_call(
        functools.partial(_resid_kernel, weight=weight),
        out_shape=jax.ShapeDtypeStruct((t, d), F32),
        grid=(t // tr,), in_specs=[row, row, vec], out_specs=row,
        compiler_params=_params("parallel"), name="residual",
    )(y, h, post_g.reshape(1, d))


def _linear_kernel(*refs, n_x, w_transposed):
    x_refs, w_ref, o_ref = refs[:n_x], refs[n_x], refs[n_x + 1]
    acc, k0 = None, 0
    for x_ref in x_refs:
        kk = x_ref.shape[1]
        if w_transposed:
            part = lax.dot_general(x_ref[...], w_ref[:, k0:k0 + kk].astype(BF16), NT, preferred_element_type=F32)
        else:
            part = jnp.dot(x_ref[...], w_ref[k0:k0 + kk, :].astype(BF16), preferred_element_type=F32)
        acc = part if acc is None else acc + part
        k0 += kk
    o_ref[...] = acc.astype(o_ref.dtype)


def _linear(xs, w, lead, col0, ncols, out_dtype, name, w_transposed=False):
    t = xs[0].shape[0]
    k = sum(x.shape[1] for x in xs)
    assert w.shape[-1 if w_transposed else -2] == k
    tm = _tile(t, (1024, 512, 256, 128, 64, 32, 16))
    tn = _tile(ncols, (512, 256, 128))
    assert col0 % tn == 0
    cb = col0 // tn
    squeezed = (None,) * len(lead)
    if w_transposed:
        w_spec = pl.BlockSpec(squeezed + (tn, k), lambda i, j: tuple(lead) + (cb + j, 0))
    else:
        w_spec = pl.BlockSpec(squeezed + (k, tn), lambda i, j: tuple(lead) + (0, cb + j))
    x_specs = [pl.BlockSpec((tm, x.shape[1]), lambda i, j: (i, 0)) for x in xs]
    return pl.pallas_call(
        functools.partial(_linear_kernel, n_x=len(xs), w_transposed=w_transposed),
        out_shape=jax.ShapeDtypeStruct((t, ncols), out_dtype),
        grid=(t // tm, ncols // tn),
        in_specs=x_specs + [w_spec],
        out_specs=pl.BlockSpec((tm, tn), lambda i, j: (i, j)),
        compiler_params=_params("parallel", "arbitrary"),
        name=name,
    )(*xs, w)


def _fused_in_kernel(*refs, has_y, weight, swiglu, w_transposed, emit_hn, n_chunks, rc):
    it = iter(refs)
    y_ref = next(it) if has_y else None
    h_ref = next(it)
    pg_ref = next(it) if has_y else None
    ng_ref = next(it)
    w_refs = [next(it) for _ in range(2 if swiglu else 1)]
    h_out = next(it) if has_y else None
    hn_out = next(it) if emit_hn else None
    o_ref = next(it)
    panels = (next(it), next(it))
    i, j = pl.program_id(0), pl.program_id(1)

    def chunk_work(panel):
        h = h_ref[...]
        if has_y:
            h = h + weight * _rms(y_ref[...], pg_ref[...])
            h_out[...] = h
        hn = _rms(h, ng_ref[...]).astype(BF16)
        if emit_hn:
            hn_out[...] = hn
        r0 = pl.multiple_of(jnp.minimum(j, n_chunks - 1) * rc, rc)
        panel[pl.ds(r0, rc), :] = hn

    def matmul(panel):
        x = panel[...]
        if swiglu:
            gate = jnp.dot(x, w_refs[0][...].astype(BF16), preferred_element_type=F32)
            up = jnp.dot(x, w_refs[1][...].astype(BF16), preferred_element_type=F32)
            out = _silu(gate) * up
        elif w_transposed:
            out = lax.dot_general(x, w_refs[0][...].astype(BF16), NT, preferred_element_type=F32)
        else:
            out = jnp.dot(x, w_refs[0][...].astype(BF16), preferred_element_type=F32)
        o_ref[...] = out.astype(o_ref.dtype)

    @pl.when(i == 0)
    def _():
        chunk_work(panels[0])

    for parity in (0, 1):
        @pl.when((i >= 1) & (i % 2 == parity))
        def _(parity=parity):
            chunk_work(panels[parity])
            matmul(panels[1 - parity])


def _fused_in(pending, h, next_g, ws, lead, ncols, out_dtype, name, swiglu=False, w_transposed=False,
              emit_hn=False):
    t, d = h.shape
    has_y = pending is not None
    tm = _tile(t, (1024, 512, 256, 128, 64, 32, 16))
    tn = _tile(ncols, (256, 128)) if swiglu else _tile(ncols, (512, 256, 128))
    n_i, n_j = t // tm, ncols // tn
    rc = next(r for r in (64, 128, 256, 512, 1024) if tm % r == 0 and tm // r <= n_j)
    n_chunks = tm // rc

    def chunk_map(i, j):
        return (jnp.where(i < n_i, i * n_chunks + jnp.minimum(j, n_chunks - 1), n_i * n_chunks - 1), 0)

    def col(i, j):
        return jnp.where(i >= 1, j, 0)

    row = pl.BlockSpec((rc, d), chunk_map)
    vec = pl.BlockSpec((1, d), lambda i, j: (0, 0))
    squeezed = (None,) * len(lead)
    if w_transposed:
        w_spec = pl.BlockSpec(squeezed + (tn, d), lambda i, j: tuple(lead) + (col(i, j), 0))
    else:
        w_spec = pl.BlockSpec(squeezed + (d, tn), lambda i, j: tuple(lead) + (0, col(i, j)))
    out_spec = pl.BlockSpec((tm, tn), lambda i, j: (jnp.maximum(i - 1, 0), col(i, j)))

    args, in_specs, out_shape, out_specs = [], [], [], []
    if has_y:
        y, post_g, weight = pending
        args += [y, h, post_g.reshape(1, d)]
        in_specs += [row, row, vec]
        out_shape.append(jax.ShapeDtypeStruct((t, d), F32))
        out_specs.append(row)
    else:
        weight = None
        args.append(h)
        in_specs.append(row)
    args.append(next_g.reshape(1, d))
    in_specs.append(vec)
    args += list(ws)
    in_specs += [w_spec] * len(ws)
    if emit_hn:
        out_shape.append(jax.ShapeDtypeStruct((t, d), BF16))
        out_specs.append(row)
    out_shape.append(jax.ShapeDtypeStruct((t, ncols), out_dtype))
    out_specs.append(out_spec)

    res = list(pl.pallas_call(
        functools.partial(_fused_in_kernel, has_y=has_y, weight=weight, swiglu=swiglu, w_transposed=w_transposed,
                          emit_hn=emit_hn, n_chunks=n_chunks, rc=rc),
        out_shape=tuple(out_shape),
        grid=(n_i + 1, n_j),
        in_specs=in_specs,
        out_specs=tuple(out_specs),
        scratch_shapes=[pltpu.VMEM((tm, d), BF16)] * 2,
        compiler_params=_params("arbitrary", "arbitrary"),
        name=name,
    )(*args))
    h_new = res.pop(0) if has_y else h
    hn = res.pop(0) if emit_hn else None
    return h_new, hn, res[0]


def _sconv_kernel(b_ref, c_ref, h_ref, w_ref, o_ref):
    u = c_ref[...] * h_ref[...]
    o_ref[...] = (b_ref[...] * _causal_conv(u, w_ref[...])).astype(o_ref.dtype)


def _short_conv(proj, conv_w, mix):
    b, s, _ = proj.shape
    tc = _tile(mix, (256, 128))
    nb = mix // tc
    blk = lambda off: pl.BlockSpec((None, s, tc), lambda bi, j: (bi, 0, off + j))
    return pl.pallas_call(
        _sconv_kernel,
        out_shape=jax.ShapeDtypeStruct((b, s, mix), BF16),
        grid=(b, nb),
        in_specs=[blk(0), blk(nb), blk(2 * nb), pl.BlockSpec((conv_w.shape[0], tc), lambda bi, j: (0, j))],
        out_specs=pl.BlockSpec((None, s, tc), lambda bi, j: (bi, 0, j)),
        compiler_params=_params("parallel", "parallel"),
        name="short_conv",
    )(proj, proj, proj, conv_w)


def _xattn_kernel(q_ref, k_ref, v_ref, o_ref):
    dh = q_ref.shape[-1] // X_HEADS
    scale = dh ** -0.5
    for hd in range(X_HEADS):
        sl = slice(hd * dh, (hd + 1) * dh)
        q = q_ref[:, sl].astype(BF16)
        k = k_ref[:, sl].astype(BF16)
        v = v_ref[:, sl].astype(BF16)
        s = lax.dot_general(q, k, NT, preferred_element_type=F32) * scale
        e = jnp.exp(s - jnp.max(s, axis=-1, keepdims=True))
        p = e / jnp.sum(e, axis=-1, keepdims=True)
        o_ref[:, sl] = jnp.dot(p.astype(BF16), v, preferred_element_type=F32).astype(o_ref.dtype)


def _cross_attention(q_arr, q_blk, kv, xw):
    b, s, _ = q_arr.shape
    m = kv.shape[1]
    tq = _tile(s, (512, 256, 128, 64))
    return pl.pallas_call(
        _xattn_kernel,
        out_shape=jax.ShapeDtypeStruct((b, s, xw), BF16),
        grid=(b, s // tq),
        in_specs=[pl.BlockSpec((None, tq, xw), lambda bi, i: (bi, i, q_blk)),
                  pl.BlockSpec((None, m, xw), lambda bi, i: (bi, 0, 0)),
                  pl.BlockSpec((None, m, xw), lambda bi, i: (bi, 0, 1))],
        out_specs=pl.BlockSpec((None, tq, xw), lambda bi, i: (bi, i, 0)),
        compiler_params=_params("parallel", "parallel"),
        name="cross_attention",
    )(q_arr, kv, kv)


def _gdn_prep_kernel(qkv_ref, cw_ref, ba_ref, alog_ref, dtb_ref,
                     qd_ref, kt_ref, vbk_ref, intra_ref, l_ref, cd_ref, prev_s, *, heads):
    d, c = GDN_HEAD_DIM, CHUNK
    mix = heads * d

    @pl.when(pl.program_id(1) == 0)
    def _():
        prev_s[...] = jnp.zeros_like(prev_s)

    cur = qkv_ref[...]
    x = _silu(_causal_conv(cur, cw_ref[...], prev_s[...]))
    prev_s[...] = cur[c - SUBLANES:, :]

    ba = ba_ref[...]
    beta = jax.nn.sigmoid(ba)
    xg = ba + dtb_ref[...]
    softplus = jnp.maximum(xg, 0.0) + jnp.log1p(jnp.exp(-jnp.abs(xg)))
    g = -jnp.exp(alog_ref[...]) * softplus
    ii = lax.broadcasted_iota(jnp.int32, (c, c), 0)
    jj = lax.broadcasted_iota(jnp.int32, (c, c), 1)
    tri, strict = ii >= jj, ii > jj
    gc = jnp.dot(tri.astype(F32), g, precision=lax.Precision.HIGHEST, preferred_element_type=F32)
    gc_t = gc.T
    lane = lax.broadcasted_iota(jnp.int32, ba.shape, 1)

    def column(arr, idx):
        return jnp.sum(jnp.where(lane == idx, arr, 0.0), axis=1, keepdims=True)

    def l2n(v):
        return v * lax.rsqrt(jnp.sum(v * v, axis=-1, keepdims=True) + EPS)

    qd, kt, vbk, intra, lmat, cd = [], [], [], [], [], []
    for h in range(heads):
        qn = l2n(x[:, h * d:(h + 1) * d]) * (d ** -0.5)
        kn = l2n(x[:, mix + h * d:mix + (h + 1) * d])
        vn = x[:, 2 * mix + h * d:2 * mix + (h + 1) * d]
        g_col = column(gc, heads + h)
        b_col = column(beta, h)
        g_row = gc_t[heads + h:heads + h + 1, :]
        g_last = g_row[:, c - 1:c]
        decay = jnp.where(tri, jnp.exp(jnp.where(tri, g_col - g_row, 0.0)), 0.0)
        kb = kn * b_col
        lhs = jnp.concatenate([kb.astype(BF16), qn.astype(BF16)], axis=0)
        prod = lax.dot_general(lhs, kn.astype(BF16), NT, preferred_element_type=F32)
        lmat.append(jnp.where(strict, prod[:c] * decay, 0.0))
        intra.append(jnp.where(tri, prod[c:] * decay, 0.0).astype(BF16))
        eg = jnp.exp(g_col)
        qd.append((qn * eg).astype(BF16))
        kt.append((kn * jnp.exp(g_last - g_col)).astype(BF16))
        vbk.append((vn * b_col).astype(BF16))
        vbk.append((kb * eg).astype(BF16))
        cd.append(jnp.broadcast_to(jnp.exp(g_last), (1, d)))
    qd_ref[...] = jnp.concatenate(qd, axis=1)
    kt_ref[...] = jnp.concatenate(kt, axis=1)
    vbk_ref[...] = jnp.concatenate(vbk, axis=1)
    intra_ref[...] = jnp.concatenate(intra, axis=1)
    l_ref[...] = jnp.concatenate(lmat, axis=1)
    cd_ref[...] = jnp.concatenate(cd, axis=1)


def _solve_kernel(l_ref, t_ref):
    c, lanes = l_ref.shape[0], t_ref.shape[2]
    t_ref[...] = jnp.zeros_like(t_ref)
    for rb in range(c // SUBLANES):
        hi = SUBLANES * (rb + 1)
        col_id = lax.broadcasted_iota(jnp.int32, (hi, lanes), 0)

        def row(rr, carry, rb=rb, hi=hi, col_id=col_id):
            r = rb * SUBLANES + rr
            acc = jnp.where(col_id == r, 1.0, 0.0)
            for mb in range(rb + 1):
                w = SUBLANES * (mb + 1)
                terms = [l_ref[r, m:m + 1, :] * t_ref[m, 0:w, :] for m in range(mb * SUBLANES, w)]
                while len(terms) > 1:
                    terms = [a + b for a, b in zip(terms[::2], terms[1::2])]
                part = acc[:w] - terms[0]
                acc = part if w == hi else jnp.concatenate([part, acc[w:]], axis=0)
            t_ref[r, 0:hi, :] = acc
            return carry

        lax.fori_loop(0, SUBLANES, row, 0)


def _gdn_scan_kernel(qd_ref, kt_ref, vbk_ref, intra_ref, t_ref, cd_ref, z_ref, ng_ref, y_ref, st, *, heads):
    d, c = GDN_HEAD_DIM, CHUNK

    @pl.when(pl.program_id(1) == 0)
    def _():
        st[...] = jnp.zeros_like(st)

    ng = ng_ref[...]
    hs = range(heads)
    sl = [slice(h * d, (h + 1) * d) for h in hs]
    sc = [slice(h * c, (h + 1) * c) for h in hs]
    uw = [jnp.dot(t_ref[:, sc[h]].astype(BF16), vbk_ref[:, 2 * h * d:2 * (h + 1) * d],
                  preferred_element_type=F32) for h in hs]
    s = [st[h] for h in hs]
    ws_qs = [jnp.dot(jnp.concatenate([uw[h][:, d:].astype(BF16), qd_ref[:, sl[h]]], axis=0), s[h].astype(BF16),
                     preferred_element_type=F32) for h in hs]
    v16 = [(uw[h][:, :d] - ws_qs[h][:c]).astype(BF16) for h in hs]
    o = [ws_qs[h][c:] + jnp.dot(intra_ref[:, sc[h]], v16[h], preferred_element_type=F32) for h in hs]
    st[...] = jnp.stack([s[h] * cd_ref[:, sl[h]] + lax.dot_general(kt_ref[:, sl[h]], v16[h], TN,
                                                                     preferred_element_type=F32) for h in hs])
    o = [o[h] * lax.rsqrt(jnp.mean(o[h] * o[h], axis=-1, keepdims=True) + EPS) * ng for h in hs]
    y_ref[...] = (jnp.concatenate(o, axis=1) * _silu(z_ref[...])).astype(y_ref.dtype)


def _gated_deltanet(proj, ba, conv_w, a_log, dt_bias, norm_g, heads):
    b, s, _ = proj.shape
    d, c = GDN_HEAD_DIM, CHUNK
    mix = heads * d
    nchunks = s // c
    alog_l = jnp.pad(a_log, (heads, LANES - 2 * heads)).reshape(1, LANES)
    dtb_l = jnp.pad(dt_bias, (heads, LANES - 2 * heads)).reshape(1, LANES)

    tok = lambda width, blk=0: pl.BlockSpec((None, c, width), lambda bi, n: (bi, n, blk))
    per_chunk = pl.BlockSpec((None, None, 1, mix), lambda bi, n: (bi, n, 0, 0))
    whole = lambda arr: pl.BlockSpec(arr.shape, lambda bi, n: (0,) * arr.ndim)
    ts = lambda width, dt: jax.ShapeDtypeStruct((b, s, width), dt)

    qd, kt, vbk, intra, lmat, cd = pl.pallas_call(
        functools.partial(_gdn_prep_kernel, heads=heads),
        out_shape=(ts(mix, BF16), ts(mix, BF16), ts(2 * mix, BF16), ts(heads * c, BF16), ts(heads * c, F32),
                   jax.ShapeDtypeStruct((b, nchunks, 1, mix), F32)),
        grid=(b, nchunks),
        in_specs=[tok(3 * mix), whole(conv_w), tok(LANES), whole(alog_l), whole(dtb_l)],
        out_specs=(tok(mix), tok(mix), tok(2 * mix), tok(heads * c), tok(heads * c), per_chunk),
        scratch_shapes=[pltpu.VMEM((SUBLANES, 3 * mix), F32)],
        compiler_params=_params("parallel", "arbitrary"),
        name="gdn_prep",
    )(proj, conv_w, ba, alog_l, dtb_l)

    nb = b * nchunks
    nbp = -(-nb // LANES) * LANES
    l_t = jnp.pad(lmat.reshape(nb, c * heads * c).T, ((0, 0), (0, nbp - nb))).reshape(c, heads * c, nbp)
    blk = pl.BlockSpec((c, c, LANES), lambda h, i: (0, h, i))
    t_t = pl.pallas_call(
        _solve_kernel,
        out_shape=jax.ShapeDtypeStruct((c, heads * c, nbp), F32),
        grid=(heads, nbp // LANES), in_specs=[blk], out_specs=blk,
        compiler_params=_params("parallel", "parallel"),
        name="gdn_solve",
    )(l_t)
    t_mat = t_t.reshape(c * heads * c, nbp)[:, :nb].T.reshape(b, s, heads * c)

    return pl.pallas_call(
        functools.partial(_gdn_scan_kernel, heads=heads),
        out_shape=ts(mix, BF16),
        grid=(b, nchunks),
        in_specs=[tok(mix), tok(mix), tok(2 * mix), tok(heads * c), tok(heads * c), per_chunk,
                  tok(mix, 3), pl.BlockSpec((1, d), lambda bi, n: (0, 0))],
        out_specs=tok(mix),
        scratch_shapes=[pltpu.VMEM((heads, d, d), F32)],
        compiler_params=_params("parallel", "arbitrary"),
        name="gdn_scan",
    )(qd, kt, vbk, intra, t_mat, cd, proj, norm_g.reshape(1, d))


def kernel(x, mem, ffn_pre_g, ffn_post_g, mix_pre_g, mix_post_g, mem_g, ffn_w_gate, ffn_w_up, ffn_w_down,
           mem_w_kv, mix_w_out, sc_w_in, sc_conv_w, gdn_w_in, gdn_conv_w, gdn_a_log, gdn_dt_bias, gdn_norm_g):
    b, s, d = x.shape
    m = mem.shape[1]
    depth = ffn_pre_g.shape[0]
    xw = mem_w_kv.shape[-1] // 2
    mix = d - xw
    heads = mix // GDN_HEAD_DIM
    t = b * s
    gdn_w_in_t = jnp.swapaxes(gdn_w_in, 1, 2)

    h = x.reshape(t, d)
    mem2 = mem.reshape(b * m, d)
    pending = None

    def ffn(h, pending, i, half):
        h, _, hidden = _fused_in(pending, h, ffn_pre_g[i, half], [ffn_w_gate, ffn_w_up], (i, half),
                                 ffn_w_gate.shape[-1], BF16, "ffn_up", swiglu=True)
        y = _linear([hidden], ffn_w_down, (i, half), 0, d, F32, "ffn_down")
        return h, (y, ffn_post_g[i, half], 0.5)

    for i in range(depth):
        j = i // N_MIXERS
        h, pending = ffn(h, pending, i, 0)

        mem_n = _norm_bf16(mem2, mem_g[i])
        kv = _linear([mem_n], mem_w_kv, (i,), 0, 2 * xw, F32, "mem_kv").reshape(b, m, 2 * xw)
        if i % N_MIXERS == 0:
            h, _, proj = _fused_in(pending, h, mix_pre_g[i], [sc_w_in], (j,), 3 * mix + xw, F32, "sc_in")
            proj = proj.reshape(b, s, 3 * mix + xw)
            y = _short_conv(proj, sc_conv_w[j], mix)
            xo = _cross_attention(proj, 3 * mix // xw, kv, xw)
        else:
            h, hn, proj = _fused_in(pending, h, mix_pre_g[i], [gdn_w_in_t], (j,), 4 * mix, F32, "gdn_in",
                                    w_transposed=True, emit_hn=True)
            w_tail = lax.slice(gdn_w_in_t, (j, 4 * mix, 0), (j + 1, gdn_w_in_t.shape[1], d))[0]
            w_ba = jnp.pad(w_tail[:2 * heads], ((0, LANES - 2 * heads), (0, 0)))
            ba = _linear([hn], w_ba, (), 0, LANES, F32, "gdn_gates", w_transposed=True)
            xq = _linear([hn], w_tail[2 * heads:], (), 0, xw, F32, "gdn_xq", w_transposed=True)
            y = _gated_deltanet(proj.reshape(b, s, 4 * mix), ba.reshape(b, s, LANES), gdn_conv_w[j],
                                gdn_a_log[j], gdn_dt_bias[j], gdn_norm_g[j], heads)
            xo = _cross_attention(xq.reshape(b, s, xw), 0, kv, xw)
        mixed = _linear([y.reshape(t, mix), xo.reshape(t, xw)], mix_w_out, (i,), 0, d, F32, "mix_out")
        pending = (mixed, mix_post_g[i], 1.0)

        h, pending = ffn(h, pending, i, 1)
    y, post_g, weight = pending
    return _residual(y, h, post_g, weight).reshape(b, s, d)
```

```python
import functools

import jax
import jax.numpy as jnp
from jax import lax
from jax.experimental import pallas as pl
from jax.experimental.pallas import tpu as pltpu

CHUNK = 64
N_MIXERS = 2
X_HEADS = 4
GDN_HEAD_DIM = 128
EPS = 1e-6
LANES = 128
SUBLANES = 8
V7X_VMEM_BYTES = 64 * 2**20
VMEM_LIMIT_BYTES = V7X_VMEM_BYTES - 6 * 2**20

F32 = jnp.float32
BF16 = jnp.bfloat16
NT = (((1,), (1,)), ((), ()))
TN = (((0,), (0,)), ((), ()))


def _tile(n, prefs):
    for p in prefs:
        if n % p == 0:
            return p
    return n


def _params(*semantics):
    return pltpu.CompilerParams(dimension_semantics=semantics, vmem_limit_bytes=VMEM_LIMIT_BYTES)


def _rms(x, g):
    return x * lax.rsqrt(jnp.mean(x * x, axis=-1, keepdims=True) + EPS) * g


def _silu(x):
    return x * jax.nn.sigmoid(x)


def _ordering_zero(v):
    bits = pltpu.bitcast(v.astype(F32), jnp.uint32)
    return ((bits >> 16) >> 16).astype(F32)


def _causal_conv(x, w, tail=None):
    width = w.shape[0]
    assert width - 1 <= SUBLANES
    row = lax.broadcasted_iota(jnp.int32, (SUBLANES, x.shape[1]), 0)
    acc = None
    for j in range(width):
        s = width - 1 - j
        if s == 0:
            xs = x
        else:
            rolled = pltpu.roll(x, s, 0)
            fill = 0.0 if tail is None else pltpu.roll(tail, s, 0)
            xs = jnp.concatenate([jnp.where(row >= s, rolled[:SUBLANES], fill), rolled[SUBLANES:]], axis=0)
        term = xs * w[j:j + 1, :]
        acc = term if acc is None else acc + term
    return acc


def _norm_kernel(x_ref, g_ref, o_ref):
    o_ref[...] = _rms(x_ref[...], g_ref[...]).astype(o_ref.dtype)


def _norm_bf16(x, g):
    t, d = x.shape
    tr = _tile(t, (256, 128, 64, 32, 16))
    return pl.pallas_call(
        _norm_kernel,
        out_shape=jax.ShapeDtypeStruct((t, d), BF16),
        grid=(t // tr,),
        in_specs=[pl.BlockSpec((tr, d), lambda i: (i, 0)), pl.BlockSpec((1, d), lambda i: (0, 0))],
        out_specs=pl.BlockSpec((tr, d), lambda i: (i, 0)),
        compiler_params=_params("parallel"),
        name="norm",
    )(x, g.reshape(1, d))


def _resid_kernel(y_ref, h_ref, pg_ref, h_out, *, weight):
    h_out[...] = h_ref[...] + weight * _rms(y_ref[...], pg_ref[...])


def _residual(y, h, post_g, weight):
    t, d = h.shape
    tr = _tile(t, (256, 128, 64, 32, 16))
    row = pl.BlockSpec((tr, d), lambda i: (i, 0))
    vec = pl.BlockSpec((1, d), lambda i: (0, 0))
    return pl.pallas_call(
        functools.partial(_resid_kernel, weight=weight),
        out_shape=jax.ShapeDtypeStruct((t, d), F32),
        grid=(t // tr,), in_specs=[row, row, vec], out_specs=row,
        compiler_params=_params("parallel"), name="residual",
    )(y, h, post_g.reshape(1, d))


def _linear_kernel(*refs, n_x, w_transposed):
    x_refs, w_ref, o_ref = refs[:n_x], refs[n_x], refs[n_x + 1]
    acc, k0 = None, 0
    for x_ref in x_refs:
        kk = x_ref.shape[1]
        if w_transposed:
            part = lax.dot_general(x_ref[...], w_ref[:, k0:k0 + kk].astype(BF16), NT, preferred_element_type=F32)
        else:
            part = jnp.dot(x_ref[...], w_ref[k0:k0 + kk, :].astype(BF16), preferred_element_type=F32)
        acc = part if acc is None else acc + part
        k0 += kk
    o_ref[...] = acc.astype(o_ref.dtype)


def _linear(xs, w, lead, col0, ncols, out_dtype, name, w_transposed=False):
    t = xs[0].shape[0]
    k = sum(x.shape[1] for x in xs)
    assert w.shape[-1 if w_transposed else -2] == k
    tm = _tile(t, (1024, 512, 256, 128, 64, 32, 16))
    tn = _tile(ncols, (512, 256, 128))
    assert col0 % tn == 0
    cb = col0 // tn
    squeezed = (None,) * len(lead)
    if w_transposed:
        w_spec = pl.BlockSpec(squeezed + (tn, k), lambda i, j: tuple(lead) + (cb + j, 0))
    else:
        w_spec = pl.BlockSpec(squeezed + (k, tn), lambda i, j: tuple(lead) + (0, cb + j))
    x_specs = [pl.BlockSpec((tm, x.shape[1]), lambda i, j: (i, 0)) for x in xs]
    return pl.pallas_call(
        functools.partial(_linear_kernel, n_x=len(xs), w_transposed=w_transposed),
        out_shape=jax.ShapeDtypeStruct((t, ncols), out_dtype),
        grid=(t // tm, ncols // tn),
        in_specs=x_specs + [w_spec],
        out_specs=pl.BlockSpec((tm, tn), lambda i, j: (i, j)),
        compiler_params=_params("parallel", "arbitrary"),
        name=name,
    )(*xs, w)


def _fused_in_kernel(*refs, has_y, weight, swiglu, w_transposed, emit_hn, n_chunks, rc):
    it = iter(refs)
    y_ref = next(it) if has_y else None
    h_ref = next(it)
    pg_ref = next(it) if has_y else None
    ng_ref = next(it)
    w_refs = [next(it) for _ in range(2 if swiglu else 1)]
    h_out = next(it) if has_y else None
    hn_out = next(it) if emit_hn else None
    o_ref = next(it)
    panels = (next(it), next(it))
    i, j = pl.program_id(0), pl.program_id(1)

    def chunk_work(panel):
        h = h_ref[...]
        if has_y:
            h = h + weight * _rms(y_ref[...], pg_ref[...])
            h_out[...] = h
        hn32 = _rms(h, ng_ref[...])
        hn = hn32.astype(BF16)
        if emit_hn:
            hn_out[...] = hn
        r0 = pl.multiple_of(jnp.minimum(j, n_chunks - 1) * rc, rc)
        panel[pl.ds(r0, rc), :] = hn
        tiles = [hn32[r:r + SUBLANES, c:c + LANES] for r in range(0, rc, SUBLANES) for c in range(0, hn32.shape[1], LANES)]
        while len(tiles) > 1:
            tiles = [a + b for a, b in zip(tiles[::2], tiles[1::2])]
        return tiles[0]

    def bf16_weight(w_ref, chunk_digest):
        zero = _ordering_zero(chunk_digest)
        if w_transposed:
            mid = (w_ref.shape[1] // 2 // LANES) * LANES
            row = (w_ref.shape[0] // 2 // SUBLANES) * SUBLANES
            rows = slice(row, row + SUBLANES)
            w = jnp.concatenate([w_ref[rows, :mid], w_ref[rows, mid:mid + LANES] + zero, w_ref[rows, mid + LANES:]],
                                axis=1)
            return jnp.concatenate([w_ref[:row, :], w, w_ref[row + SUBLANES:, :]], axis=0).astype(BF16)
        mid = (w_ref.shape[0] // 2 // SUBLANES) * SUBLANES
        zero = jnp.concatenate([zero] * (w_ref.shape[1] // LANES), axis=1)
        return jnp.concatenate([w_ref[:mid, :], w_ref[mid:mid + SUBLANES, :] + zero, w_ref[mid + SUBLANES:, :]],
                               axis=0).astype(BF16)

    def matmul(panel, chunk_digest):
        x = panel[...]
        if swiglu:
            gate = jnp.dot(x, bf16_weight(w_refs[0], chunk_digest), preferred_element_type=F32)
            up = jnp.dot(x, bf16_weight(w_refs[1], chunk_digest), preferred_element_type=F32)
            out = _silu(gate) * up
        elif w_transposed:
            out = lax.dot_general(x, bf16_weight(w_refs[0], chunk_digest), NT, preferred_element_type=F32)
        else:
            out = jnp.dot(x, bf16_weight(w_refs[0], chunk_digest), preferred_element_type=F32)
        o_ref[...] = out.astype(o_ref.dtype)

    @pl.when(i == 0)
    def _():
        chunk_work(panels[0])

    for parity in (0, 1):
        @pl.when((i >= 1) & (i % 2 == parity))
        def _(parity=parity):
            matmul(panels[1 - parity], chunk_work(panels[parity]))


def _fused_in(pending, h, next_g, ws, lead, ncols, out_dtype, name, swiglu=False, w_transposed=False,
              emit_hn=False):
    t, d = h.shape
    has_y = pending is not None
    tm = _tile(t, (1024, 512, 256, 128, 64, 32, 16))
    tn = _tile(ncols, (256, 128)) if swiglu else _tile(ncols, (512, 256, 128))
    n_i, n_j = t // tm, ncols // tn
    rc = next(r for r in (64, 128, 256, 512, 1024) if tm % r == 0 and tm // r <= n_j)
    n_chunks = tm // rc

    def chunk_map(i, j):
        return (jnp.where(i < n_i, i * n_chunks + jnp.minimum(j, n_chunks - 1), n_i * n_chunks - 1), 0)

    def col(i, j):
        return jnp.where(i >= 1, j, 0)

    row = pl.BlockSpec((rc, d), chunk_map)
    vec = pl.BlockSpec((1, d), lambda i, j: (0, 0))
    squeezed = (None,) * len(lead)
    if w_transposed:
        w_spec = pl.BlockSpec(squeezed + (tn, d), lambda i, j: tuple(lead) + (col(i, j), 0))
    else:
        w_spec = pl.BlockSpec(squeezed + (d, tn), lambda i, j: tuple(lead) + (0, col(i, j)))
    out_spec = pl.BlockSpec((tm, tn), lambda i, j: (jnp.maximum(i - 1, 0), col(i, j)))

    args, in_specs, out_shape, out_specs = [], [], [], []
    if has_y:
        y, post_g, weight = pending
        args += [y, h, post_g.reshape(1, d)]
        in_specs += [row, row, vec]
        out_shape.append(jax.ShapeDtypeStruct((t, d), F32))
        out_specs.append(row)
    else:
        weight = None
        args.append(h)
        in_specs.append(row)
    args.append(next_g.reshape(1, d))
    in_specs.append(vec)
    args += list(ws)
    in_specs += [w_spec] * len(ws)
    if emit_hn:
        out_shape.append(jax.ShapeDtypeStruct((t, d), BF16))
        out_specs.append(row)
    out_shape.append(jax.ShapeDtypeStruct((t, ncols), out_dtype))
    out_specs.append(out_spec)

    res = list(pl.pallas_call(
        functools.partial(_fused_in_kernel, has_y=has_y, weight=weight, swiglu=swiglu, w_transposed=w_transposed,
                          emit_hn=emit_hn, n_chunks=n_chunks, rc=rc),
        out_shape=tuple(out_shape),
        grid=(n_i + 1, n_j),
        in_specs=in_specs,
        out_specs=tuple(out_specs),
        scratch_shapes=[pltpu.VMEM((tm, d), BF16)] * 2,
        compiler_params=_params("arbitrary", "arbitrary"),
        name=name,
    )(*args))
    h_new = res.pop(0) if has_y else h
    hn = res.pop(0) if emit_hn else None
    return h_new, hn, res[0]


def _sconv_kernel(b_ref, c_ref, h_ref, w_ref, o_ref):
    u = c_ref[...] * h_ref[...]
    o_ref[...] = (b_ref[...] * _causal_conv(u, w_ref[...])).astype(o_ref.dtype)


def _short_conv(proj, conv_w, mix):
    b, s, _ = proj.shape
    tc = _tile(mix, (256, 128))
    nb = mix // tc
    blk = lambda off: pl.BlockSpec((None, s, tc), lambda bi, j: (bi, 0, off + j))
    return pl.pallas_call(
        _sconv_kernel,
        out_shape=jax.ShapeDtypeStruct((b, s, mix), BF16),
        grid=(b, nb),
        in_specs=[blk(0), blk(nb), blk(2 * nb), pl.BlockSpec((conv_w.shape[0], tc), lambda bi, j: (0, j))],
        out_specs=pl.BlockSpec((None, s, tc), lambda bi, j: (bi, 0, j)),
        compiler_params=_params("parallel", "parallel"),
        name="short_conv",
    )(proj, proj, proj, conv_w)


def _xattn_kernel(q_ref, k_ref, v_ref, o_ref):
    dh = q_ref.shape[-1] // X_HEADS
    scale = dh ** -0.5
    for hd in range(X_HEADS):
        sl = slice(hd * dh, (hd + 1) * dh)
        q = q_ref[:, sl].astype(BF16)
        k = k_ref[:, sl].astype(BF16)
        v = v_ref[:, sl].astype(BF16)
        s = lax.dot_general(q, k, NT, preferred_element_type=F32) * scale
        e = jnp.exp(s - jnp.max(s, axis=-1, keepdims=True))
        p = e / jnp.sum(e, axis=-1, keepdims=True)
        o_ref[:, sl] = jnp.dot(p.astype(BF16), v, preferred_element_type=F32).astype(o_ref.dtype)


def _cross_attention(q_arr, q_blk, kv, xw):
    b, s, _ = q_arr.shape
    m = kv.shape[1]
    tq = _tile(s, (512, 256, 128, 64))
    return pl.pallas_call(
        _xattn_kernel,
        out_shape=jax.ShapeDtypeStruct((b, s, xw), BF16),
        grid=(b, s // tq),
        in_specs=[pl.BlockSpec((None, tq, xw), lambda bi, i: (bi, i, q_blk)),
                  pl.BlockSpec((None, m, xw), lambda bi, i: (bi, 0, 0)),
                  pl.BlockSpec((None, m, xw), lambda bi, i: (bi, 0, 1))],
        out_specs=pl.BlockSpec((None, tq, xw), lambda bi, i: (bi, i, 0)),
        compiler_params=_params("parallel", "parallel"),
        name="cross_attention",
    )(q_arr, kv, kv)


def _gdn_prep_kernel(qkv_ref, cw_ref, ba_ref, alog_ref, dtb_ref,
                     qd_ref, kt_ref, vbk_ref, intra_ref, l_ref, cd_ref, prev_s, *, heads):
    d, c = GDN_HEAD_DIM, CHUNK
    mix = heads * d

    @pl.when(pl.program_id(1) == 0)
    def _():
        prev_s[...] = jnp.zeros_like(prev_s)

    cur = qkv_ref[...]
    x = _silu(_causal_conv(cur, cw_ref[...], prev_s[...]))
    prev_s[...] = cur[c - SUBLANES:, :]

    ba = ba_ref[...]
    beta = jax.nn.sigmoid(ba)
    xg = ba + dtb_ref[...]
    softplus = jnp.maximum(xg, 0.0) + jnp.log1p(jnp.exp(-jnp.abs(xg)))
    g = -jnp.exp(alog_ref[...]) * softplus
    ii = lax.broadcasted_iota(jnp.int32, (c, c), 0)
    jj = lax.broadcasted_iota(jnp.int32, (c, c), 1)
    tri, strict = ii >= jj, ii > jj
    gc = jnp.dot(tri.astype(F32), g, precision=lax.Precision.HIGHEST, preferred_element_type=F32)
    gc_t = gc.T
    lane = lax.broadcasted_iota(jnp.int32, ba.shape, 1)

    def column(arr, idx):
        return jnp.sum(jnp.where(lane == idx, arr, 0.0), axis=1, keepdims=True)

    def l2n(v):
        return v * lax.rsqrt(jnp.sum(v * v, axis=-1, keepdims=True) + EPS)

    qd, kt, vbk, intra, lmat, cd = [], [], [], [], [], []
    for h in range(heads):
        qn = l2n(x[:, h * d:(h + 1) * d]) * (d ** -0.5)
        kn = l2n(x[:, mix + h * d:mix + (h + 1) * d])
        vn = x[:, 2 * mix + h * d:2 * mix + (h + 1) * d]
        g_col = column(gc, heads + h)
        b_col = column(beta, h)
        g_row = gc_t[heads + h:heads + h + 1, :]
        g_last = g_row[:, c - 1:c]
        decay = jnp.where(tri, jnp.exp(jnp.where(tri, g_col - g_row, 0.0)), 0.0)
        kb = kn * b_col
        lhs = jnp.concatenate([kb.astype(BF16), qn.astype(BF16)], axis=0)
        prod = lax.dot_general(lhs, kn.astype(BF16), NT, preferred_element_type=F32)
        lmat.append(jnp.where(strict, prod[:c] * decay, 0.0))
        intra.append(jnp.where(tri, prod[c:] * decay, 0.0).astype(BF16))
        eg = jnp.exp(g_col)
        qd.append((qn * eg).astype(BF16))
        kt.append((kn * jnp.exp(g_last - g_col)).astype(BF16))
        vbk.append((vn * b_col).astype(BF16))
        vbk.append((kb * eg).astype(BF16))
        cd.append(jnp.broadcast_to(jnp.exp(g_last), (1, d)))
    qd_ref[...] = jnp.concatenate(qd, axis=1)
    kt_ref[...] = jnp.concatenate(kt, axis=1)
    vbk_ref[...] = jnp.concatenate(vbk, axis=1)
    intra_ref[...] = jnp.concatenate(intra, axis=1)
    l_ref[...] = jnp.concatenate(lmat, axis=1)
    cd_ref[...] = jnp.concatenate(cd, axis=1)


def _solve_kernel(l_ref, t_ref):
    c, lanes = l_ref.shape[0], t_ref.shape[2]
    t_ref[...] = jnp.zeros_like(t_ref)
    for rb in range(c // SUBLANES):
        hi = SUBLANES * (rb + 1)
        col_id = lax.broadcasted_iota(jnp.int32, (hi, lanes), 0)

        def row(rr, carry, rb=rb, hi=hi, col_id=col_id):
            r = rb * SUBLANES + rr
            acc = jnp.where(col_id == r, 1.0, 0.0)
            for mb in range(rb + 1):
                w = SUBLANES * (mb + 1)
                terms = [l_ref[r, m:m + 1, :] * t_ref[m, 0:w, :] for m in range(mb * SUBLANES, w)]
                while len(terms) > 1:
                    terms = [a + b for a, b in zip(terms[::2], terms[1::2])]
                part = acc[:w] - terms[0]
                acc = part if w == hi else jnp.concatenate([part, acc[w:]], axis=0)
            t_ref[r, 0:hi, :] = acc
            return carry

        lax.fori_loop(0, SUBLANES, row, 0)


def _gdn_scan_kernel(qd_ref, kt_ref, vbk_ref, intra_ref, t_ref, cd_ref, z_ref, ng_ref, y_ref, st, *, heads):
    d, c = GDN_HEAD_DIM, CHUNK

    @pl.when(pl.program_id(1) == 0)
    def _():
        st[...] = jnp.zeros_like(st)

    ng = ng_ref[...]
    hs = range(heads)
    sl = [slice(h * d, (h + 1) * d) for h in hs]
    sc = [slice(h * c, (h + 1) * c) for h in hs]
    uw = [jnp.dot(t_ref[:, sc[h]].astype(BF16), vbk_ref[:, 2 * h * d:2 * (h + 1) * d],
                  preferred_element_type=F32) for h in hs]
    s = [st[h] for h in hs]
    ws_qs = [jnp.dot(jnp.concatenate([uw[h][:, d:].astype(BF16), qd_ref[:, sl[h]]], axis=0), s[h].astype(BF16),
                     preferred_element_type=F32) for h in hs]
    v16 = [(uw[h][:, :d] - ws_qs[h][:c]).astype(BF16) for h in hs]
    o = [ws_qs[h][c:] + jnp.dot(intra_ref[:, sc[h]], v16[h], preferred_element_type=F32) for h in hs]
    st[...] = jnp.stack([s[h] * cd_ref[:, sl[h]] + lax.dot_general(kt_ref[:, sl[h]], v16[h], TN,
                                                                     preferred_element_type=F32) for h in hs])
    o = [o[h] * lax.rsqrt(jnp.mean(o[h] * o[h], axis=-1, keepdims=True) + EPS) * ng for h in hs]
    y_ref[...] = (jnp.concatenate(o, axis=1) * _silu(z_ref[...])).astype(y_ref.dtype)


def _gated_deltanet(proj, ba, conv_w, a_log, dt_bias, norm_g, heads):
    b, s, _ = proj.shape
    d, c = GDN_HEAD_DIM, CHUNK
    mix = heads * d
    nchunks = s // c
    alog_l = jnp.pad(a_log, (heads, LANES - 2 * heads)).reshape(1, LANES)
    dtb_l = jnp.pad(dt_bias, (heads, LANES - 2 * heads)).reshape(1, LANES)

    tok = lambda width, blk=0: pl.BlockSpec((None, c, width), lambda bi, n: (bi, n, blk))
    per_chunk = pl.BlockSpec((None, None, 1, mix), lambda bi, n: (bi, n, 0, 0))
    whole = lambda arr: pl.BlockSpec(arr.shape, lambda bi, n: (0,) * arr.ndim)
    ts = lambda width, dt: jax.ShapeDtypeStruct((b, s, width), dt)

    qd, kt, vbk, intra, lmat, cd = pl.pallas_call(
        functools.partial(_gdn_prep_kernel, heads=heads),
        out_shape=(ts(mix, BF16), ts(mix, BF16), ts(2 * mix, BF16), ts(heads * c, BF16), ts(heads * c, F32),
                   jax.ShapeDtypeStruct((b, nchunks, 1, mix), F32)),
        grid=(b, nchunks),
        in_specs=[tok(3 * mix), whole(conv_w), tok(LANES), whole(alog_l), whole(dtb_l)],
        out_specs=(tok(mix), tok(mix), tok(2 * mix), tok(heads * c), tok(heads * c), per_chunk),
        scratch_shapes=[pltpu.VMEM((SUBLANES, 3 * mix), F32)],
        compiler_params=_params("parallel", "arbitrary"),
        name="gdn_prep",
    )(proj, conv_w, ba, alog_l, dtb_l)

    nb = b * nchunks
    nbp = -(-nb // LANES) * LANES
    l_t = jnp.pad(lmat.reshape(nb, c * heads * c).T, ((0, 0), (0, nbp - nb))).reshape(c, heads * c, nbp)
    blk = pl.BlockSpec((c, c, LANES), lambda h, i: (0, h, i))
    t_t = pl.pallas_call(
        _solve_kernel,
        out_shape=jax.ShapeDtypeStruct((c, heads * c, nbp), F32),
        grid=(heads, nbp // LANES), in_specs=[blk], out_specs=blk,
        compiler_params=_params("parallel", "parallel"),
        name="gdn_solve",
    )(l_t)
    t_mat = t_t.reshape(c * heads * c, nbp)[:, :nb].T.reshape(b, s, heads * c)

    return pl.pallas_call(
        functools.partial(_gdn_scan_kernel, heads=heads),
        out_shape=ts(mix, BF16),
        grid=(b, nchunks),
        in_specs=[tok(mix), tok(mix), tok(2 * mix), tok(heads * c), tok(heads * c), per_chunk,
                  tok(mix, 3), pl.BlockSpec((1, d), lambda bi, n: (0, 0))],
        out_specs=tok(mix),
        scratch_shapes=[pltpu.VMEM((heads, d, d), F32)],
        compiler_params=_params("parallel", "arbitrary"),
        name="gdn_scan",
    )(qd, kt, vbk, intra, t_mat, cd, proj, norm_g.reshape(1, d))


def kernel(x, mem, ffn_pre_g, ffn_post_g, mix_pre_g, mix_post_g, mem_g, ffn_w_gate, ffn_w_up, ffn_w_down,
           mem_w_kv, mix_w_out, sc_w_in, sc_conv_w, gdn_w_in, gdn_conv_w, gdn_a_log, gdn_dt_bias, gdn_norm_g):
    b, s, d = x.shape
    m = mem.shape[1]
    depth = ffn_pre_g.shape[0]
    xw = mem_w_kv.shape[-1] // 2
    mix = d - xw
    heads = mix // GDN_HEAD_DIM
    t = b * s
    gdn_w_in_t = jnp.swapaxes(gdn_w_in, 1, 2)

    h = x.reshape(t, d)
    mem2 = mem.reshape(b * m, d)
    pending = None

    def ffn(h, pending, i, half):
        h, _, hidden = _fused_in(pending, h, ffn_pre_g[i, half], [ffn_w_gate, ffn_w_up], (i, half),
                                 ffn_w_gate.shape[-1], BF16, "ffn_up", swiglu=True)
        y = _linear([hidden], ffn_w_down, (i, half), 0, d, F32, "ffn_down")
        return h, (y, ffn_post_g[i, half], 0.5)

    for i in range(depth):
        j = i // N_MIXERS
        h, pending = ffn(h, pending, i, 0)

        mem_n = _norm_bf16(mem2, mem_g[i])
        kv = _linear([mem_n], mem_w_kv, (i,), 0, 2 * xw, F32, "mem_kv").reshape(b, m, 2 * xw)
        if i % N_MIXERS == 0:
            h, _, proj = _fused_in(pending, h, mix_pre_g[i], [sc_w_in], (j,), 3 * mix + xw, F32, "sc_in")
            proj = proj.reshape(b, s, 3 * mix + xw)
            y = _short_conv(proj, sc_conv_w[j], mix)
            xo = _cross_attention(proj, 3 * mix // xw, kv, xw)
        else:
            h, hn, proj = _fused_in(pending, h, mix_pre_g[i], [gdn_w_in_t], (j,), 4 * mix, F32, "gdn_in",
                                    w_transposed=True, emit_hn=True)
            w_tail = lax.slice(gdn_w_in_t, (j, 4 * mix, 0), (j + 1, gdn_w_in_t.shape[1], d))[0]
            w_ba = jnp.pad(w_tail[:2 * heads], ((0, LANES - 2 * heads), (0, 0)))
            ba = _linear([hn], w_ba, (), 0, LANES, F32, "gdn_gates", w_transposed=True)
            xq = _linear([hn], w_tail[2 * heads:], (), 0, xw, F32, "gdn_xq", w_transposed=True)
            y = _gated_deltanet(proj.reshape(b, s, 4 * mix), ba.reshape(b, s, LANES), gdn_conv_w[j],
                                gdn_a_log[j], gdn_dt_bias[j], gdn_norm_g[j], heads)
            xo = _cross_attention(xq.reshape(b, s, xw), 0, kv, xw)
        mixed = _linear([y.reshape(t, mix), xo.reshape(t, xw)], mix_w_out, (i,), 0, d, F32, "mix_out")
        pending = (mixed, mix_post_g[i], 1.0)

        h, pending = ffn(h, pending, i, 1)
    y, post_g, weight = pending
    return _residual(y, h, post_g, weight).reshape(b, s, d)
```

```python
import functools

import jax
import jax.numpy as jnp
from jax import lax
from jax.experimental import pallas as pl
from jax.experimental.pallas import tpu as pltpu

CHUNK = 64
N_MIXERS = 2
X_HEADS = 4
GDN_HEAD_DIM = 128
EPS = 1e-6
LANES = 128
SUBLANES = 8
V7X_VMEM_BYTES = 64 * 2**20
VMEM_LIMIT_BYTES = V7X_VMEM_BYTES - 6 * 2**20

F32 = jnp.float32
BF16 = jnp.bfloat16
NT = (((1,), (1,)), ((), ()))
TN = (((0,), (0,)), ((), ()))


def _tile(n, prefs):
    for p in prefs:
        if n % p == 0:
            return p
    return n


def _params(*semantics):
    return pltpu.CompilerParams(dimension_semantics=semantics, vmem_limit_bytes=VMEM_LIMIT_BYTES)


def _rms(x, g):
    return x * lax.rsqrt(jnp.mean(x * x, axis=-1, keepdims=True) + EPS) * g


def _silu(x):
    return x * jax.nn.sigmoid(x)


def _ordering_zero(v):
    bits = pltpu.bitcast(v.astype(F32), jnp.uint32)
    return ((bits >> 16) >> 16).astype(F32)


def _causal_conv(x, w, tail=None):
    width = w.shape[0]
    assert width - 1 <= SUBLANES
    row = lax.broadcasted_iota(jnp.int32, (SUBLANES, x.shape[1]), 0)
    acc = None
    for j in range(width):
        s = width - 1 - j
        if s == 0:
            xs = x
        else:
            rolled = pltpu.roll(x, s, 0)
            fill = 0.0 if tail is None else pltpu.roll(tail, s, 0)
            xs = jnp.concatenate([jnp.where(row >= s, rolled[:SUBLANES], fill), rolled[SUBLANES:]], axis=0)
        term = xs * w[j:j + 1, :]
        acc = term if acc is None else acc + term
    return acc


def _norm_kernel(x_ref, g_ref, o_ref):
    o_ref[...] = _rms(x_ref[...], g_ref[...]).astype(o_ref.dtype)


def _norm_bf16(x, g):
    t, d = x.shape
    tr = _tile(t, (256, 128, 64, 32, 16))
    return pl.pallas_call(
        _norm_kernel,
        out_shape=jax.ShapeDtypeStruct((t, d), BF16),
        grid=(t // tr,),
        in_specs=[pl.BlockSpec((tr, d), lambda i: (i, 0)), pl.BlockSpec((1, d), lambda i: (0, 0))],
        out_specs=pl.BlockSpec((tr, d), lambda i: (i, 0)),
        compiler_params=_params("parallel"),
        name="norm",
    )(x, g.reshape(1, d))


def _resid_kernel(y_ref, h_ref, pg_ref, h_out, *, weight):
    h_out[...] = h_ref[...] + weight * _rms(y_ref[...], pg_ref[...])


def _residual(y, h, post_g, weight):
    t, d = h.shape
    tr = _tile(t, (256, 128, 64, 32, 16))
    row = pl.BlockSpec((tr, d), lambda i: (i, 0))
    vec = pl.BlockSpec((1, d), lambda i: (0, 0))
    return pl.pallas_call(
        functools.partial(_resid_kernel, weight=weight),
        out_shape=jax.ShapeDtypeStruct((t, d), F32),
        grid=(t // tr,), in_specs=[row, row, vec], out_specs=row,
        compiler_params=_params("parallel"), name="residual",
    )(y, h, post_g.reshape(1, d))


def _linear_kernel(x_ref, w_ref, o_ref, *, w_transposed):
    if w_transposed:
        out = lax.dot_general(x_ref[...], w_ref[...].astype(BF16), NT, preferred_element_type=F32)
    else:
        out = jnp.dot(x_ref[...], w_ref[...].astype(BF16), preferred_element_type=F32)
    o_ref[...] = out.astype(o_ref.dtype)


def _linear(x, w, lead, ncols, out_dtype, name, w_transposed=False):
    t, k = x.shape
    assert w.shape[-1 if w_transposed else -2] == k
    tm = _tile(t, (1024, 512, 256, 128, 64, 32, 16))
    tn = _tile(ncols, (512, 256, 128))
    squeezed = (None,) * len(lead)
    if w_transposed:
        w_spec = pl.BlockSpec(squeezed + (tn, k), lambda i, j: tuple(lead) + (j, 0))
    else:
        w_spec = pl.BlockSpec(squeezed + (k, tn), lambda i, j: tuple(lead) + (0, j))
    return pl.pallas_call(
        functools.partial(_linear_kernel, w_transposed=w_transposed),
        out_shape=jax.ShapeDtypeStruct((t, ncols), out_dtype),
        grid=(t // tm, ncols // tn),
        in_specs=[pl.BlockSpec((tm, k), lambda i, j: (i, 0)), w_spec],
        out_specs=pl.BlockSpec((tm, tn), lambda i, j: (i, j)),
        compiler_params=_params("parallel", "arbitrary"),
        name=name,
    )(x, w)


def _fused_in_kernel(*refs, has_y, weight, swiglu, w_transposed, emit_hn, n_chunks, rc):
    it = iter(refs)
    y_ref = next(it) if has_y else None
    h_ref = next(it)
    pg_ref = next(it) if has_y else None
    ng_ref = next(it)
    w_refs = [next(it) for _ in range(2 if swiglu else 1)]
    h_out = next(it) if has_y else None
    hn_out = next(it) if emit_hn else None
    o_ref = next(it)
    panels = (next(it), next(it))
    i, j = pl.program_id(0), pl.program_id(1)

    def chunk_work(panel):
        h = h_ref[...]
        if has_y:
            h = h + weight * _rms(y_ref[...], pg_ref[...])
            h_out[...] = h
        hn32 = _rms(h, ng_ref[...])
        hn = hn32.astype(BF16)
        if emit_hn:
            hn_out[...] = hn
        r0 = pl.multiple_of(jnp.minimum(j, n_chunks - 1) * rc, rc)
        panel[pl.ds(r0, rc), :] = hn
        tiles = [hn32[r:r + SUBLANES, c:c + LANES]
                 for r in range(0, rc, SUBLANES) for c in range(0, hn32.shape[1], LANES)]
        while len(tiles) > 1:
            tiles = [a + b for a, b in zip(tiles[::2], tiles[1::2])]
        return tiles[0]

    def bf16_weight(w_ref, chunk_digest):
        zero = _ordering_zero(chunk_digest)
        if w_transposed:
            mid = (w_ref.shape[1] // 2 // LANES) * LANES
            row = (w_ref.shape[0] // 2 // SUBLANES) * SUBLANES
            rows = slice(row, row + SUBLANES)
            w = jnp.concatenate([w_ref[rows, :mid], w_ref[rows, mid:mid + LANES] + zero, w_ref[rows, mid + LANES:]],
                                axis=1)
            return jnp.concatenate([w_ref[:row, :], w, w_ref[row + SUBLANES:, :]], axis=0).astype(BF16)
        mid = (w_ref.shape[0] // 2 // SUBLANES) * SUBLANES
        zero = jnp.concatenate([zero] * (w_ref.shape[1] // LANES), axis=1)
        return jnp.concatenate([w_ref[:mid, :], w_ref[mid:mid + SUBLANES, :] + zero, w_ref[mid + SUBLANES:, :]],
                               axis=0).astype(BF16)

    def matmul(panel, chunk_digest):
        x = panel[...]
        if swiglu:
            gate = jnp.dot(x, bf16_weight(w_refs[0], chunk_digest), preferred_element_type=F32)
            up = jnp.dot(x, bf16_weight(w_refs[1], chunk_digest), preferred_element_type=F32)
            out = _silu(gate) * up
        elif w_transposed:
            out = lax.dot_general(x, bf16_weight(w_refs[0], chunk_digest), NT, preferred_element_type=F32)
        else:
            out = jnp.dot(x, bf16_weight(w_refs[0], chunk_digest), preferred_element_type=F32)
        o_ref[...] = out.astype(o_ref.dtype)

    @pl.when(i == 0)
    def _():
        chunk_work(panels[0])

    for parity in (0, 1):
        @pl.when((i >= 1) & (i % 2 == parity))
        def _(parity=parity):
            matmul(panels[1 - parity], chunk_work(panels[parity]))


def _fused_in(pending, h, next_g, ws, lead, ncols, out_dtype, name, swiglu=False, w_transposed=False,
              emit_hn=False):
    t, d = h.shape
    has_y = pending is not None
    tm = _tile(t, (1024, 512, 256, 128, 64, 32, 16))
    tn = _tile(ncols, (256, 128)) if swiglu else _tile(ncols, (512, 256, 128))
    n_i, n_j = t // tm, ncols // tn
    rc = next(r for r in (64, 128, 256, 512, 1024) if tm % r == 0 and tm // r <= n_j)
    n_chunks = tm // rc

    def chunk_map(i, j):
        return (jnp.where(i < n_i, i * n_chunks + jnp.minimum(j, n_chunks - 1), n_i * n_chunks - 1), 0)

    def col(i, j):
        return jnp.where(i >= 1, j, 0)

    row = pl.BlockSpec((rc, d), chunk_map)
    vec = pl.BlockSpec((1, d), lambda i, j: (0, 0))
    squeezed = (None,) * len(lead)
    if w_transposed:
        w_spec = pl.BlockSpec(squeezed + (tn, d), lambda i, j: tuple(lead) + (col(i, j), 0))
    else:
        w_spec = pl.BlockSpec(squeezed + (d, tn), lambda i, j: tuple(lead) + (0, col(i, j)))
    out_spec = pl.BlockSpec((tm, tn), lambda i, j: (jnp.maximum(i - 1, 0), col(i, j)))

    args, in_specs, out_shape, out_specs = [], [], [], []
    if has_y:
        y, post_g, weight = pending
        args += [y, h, post_g.reshape(1, d)]
        in_specs += [row, row, vec]
        out_shape.append(jax.ShapeDtypeStruct((t, d), F32))
        out_specs.append(row)
    else:
        weight = None
        args.append(h)
        in_specs.append(row)
    args.append(next_g.reshape(1, d))
    in_specs.append(vec)
    args += list(ws)
    in_specs += [w_spec] * len(ws)
    if emit_hn:
        out_shape.append(jax.ShapeDtypeStruct((t, d), BF16))
        out_specs.append(row)
    out_shape.append(jax.ShapeDtypeStruct((t, ncols), out_dtype))
    out_specs.append(out_spec)

    res = list(pl.pallas_call(
        functools.partial(_fused_in_kernel, has_y=has_y, weight=weight, swiglu=swiglu, w_transposed=w_transposed,
                          emit_hn=emit_hn, n_chunks=n_chunks, rc=rc),
        out_shape=tuple(out_shape),
        grid=(n_i + 1, n_j),
        in_specs=in_specs,
        out_specs=tuple(out_specs),
        scratch_shapes=[pltpu.VMEM((tm, d), BF16)] * 2,
        compiler_params=_params("arbitrary", "arbitrary"),
        name=name,
    )(*args))
    h_new = res.pop(0) if has_y else h
    hn = res.pop(0) if emit_hn else None
    return h_new, hn, res[0]


def _sconv_kernel(b_ref, c_ref, h_ref, w_ref, o_ref):
    u = c_ref[...] * h_ref[...]
    o_ref[...] = (b_ref[...] * _causal_conv(u, w_ref[...])).astype(o_ref.dtype)


def _short_conv(proj, conv_w, mix, width):
    b, s, _ = proj.shape
    tc = _tile(mix, (256, 128))
    nb = mix // tc
    blk = lambda off: pl.BlockSpec((None, s, tc), lambda bi, j: (bi, 0, off + j))
    return pl.pallas_call(
        _sconv_kernel,
        out_shape=jax.ShapeDtypeStruct((b, s, width), BF16),
        grid=(b, nb),
        in_specs=[blk(0), blk(nb), blk(2 * nb), pl.BlockSpec((conv_w.shape[0], tc), lambda bi, j: (0, j))],
        out_specs=pl.BlockSpec((None, s, tc), lambda bi, j: (bi, 0, j)),
        compiler_params=_params("parallel", "parallel"),
        name="short_conv",
    )(proj, proj, proj, conv_w)


def _xattn_kernel(q_ref, k_ref, v_ref, _mixed_ref, o_ref):
    dh = q_ref.shape[-1] // X_HEADS
    scale = dh ** -0.5
    for hd in range(X_HEADS):
        sl = slice(hd * dh, (hd + 1) * dh)
        q = q_ref[:, sl].astype(BF16)
        k = k_ref[:, sl].astype(BF16)
        v = v_ref[:, sl].astype(BF16)
        s = lax.dot_general(q, k, NT, preferred_element_type=F32) * scale
        e = jnp.exp(s - jnp.max(s, axis=-1, keepdims=True))
        p = e / jnp.sum(e, axis=-1, keepdims=True)
        o_ref[:, sl] = jnp.dot(p.astype(BF16), v, preferred_element_type=F32).astype(o_ref.dtype)


def _cross_attention(q_arr, q_blk, kv, xw, mixed):
    b, s, _ = q_arr.shape
    out_blk = mixed.shape[-1] // xw - 1
    m = kv.shape[1]
    tq = _tile(s, (512, 256, 128, 64))
    return pl.pallas_call(
        _xattn_kernel,
        out_shape=jax.ShapeDtypeStruct(mixed.shape, mixed.dtype),
        grid=(b, s // tq),
        in_specs=[pl.BlockSpec((None, tq, xw), lambda bi, i: (bi, i, q_blk)),
                  pl.BlockSpec((None, m, xw), lambda bi, i: (bi, 0, 0)),
                  pl.BlockSpec((None, m, xw), lambda bi, i: (bi, 0, 1)),
                  pl.BlockSpec(memory_space=pl.ANY)],
        out_specs=pl.BlockSpec((None, tq, xw), lambda bi, i: (bi, i, out_blk)),
        input_output_aliases={3: 0},
        compiler_params=_params("parallel", "parallel"),
        name="cross_attention",
    )(q_arr, kv, kv, mixed)


def _gdn_prep_kernel(qkv_ref, cw_ref, ba_ref, alog_ref, dtb_ref,
                     qd_ref, kt_ref, vbk_ref, intra_ref, l_ref, cd_ref, prev_s, *, heads):
    d, c = GDN_HEAD_DIM, CHUNK
    mix = heads * d

    @pl.when(pl.program_id(1) == 0)
    def _():
        prev_s[...] = jnp.zeros_like(prev_s)

    cur = qkv_ref[...]
    x = _silu(_causal_conv(cur, cw_ref[...], prev_s[...]))
    prev_s[...] = cur[c - SUBLANES:, :]

    ba = ba_ref[...]
    beta = jax.nn.sigmoid(ba)
    xg = ba + dtb_ref[...]
    softplus = jnp.maximum(xg, 0.0) + jnp.log1p(jnp.exp(-jnp.abs(xg)))
    g = -jnp.exp(alog_ref[...]) * softplus
    ii = lax.broadcasted_iota(jnp.int32, (c, c), 0)
    jj = lax.broadcasted_iota(jnp.int32, (c, c), 1)
    tri, strict = ii >= jj, ii > jj
    gc = jnp.dot(tri.astype(F32), g, precision=lax.Precision.HIGHEST, preferred_element_type=F32)
    gc_t = gc.T
    lane = lax.broadcasted_iota(jnp.int32, ba.shape, 1)

    def column(arr, idx):
        return jnp.sum(jnp.where(lane == idx, arr, 0.0), axis=1, keepdims=True)

    def l2n(v):
        return v * lax.rsqrt(jnp.sum(v * v, axis=-1, keepdims=True) + EPS)

    qd, kt, vbk, intra, lmat, cd = [], [], [], [], [], []
    for h in range(heads):
        qn = l2n(x[:, h * d:(h + 1) * d]) * (d ** -0.5)
        kn = l2n(x[:, mix + h * d:mix + (h + 1) * d])
        vn = x[:, 2 * mix + h * d:2 * mix + (h + 1) * d]
        g_col = column(gc, heads + h)
        b_col = column(beta, h)
        g_row = gc_t[heads + h:heads + h + 1, :]
        g_last = g_row[:, c - 1:c]
        decay = jnp.where(tri, jnp.exp(jnp.where(tri, g_col - g_row, 0.0)), 0.0)
        kb = kn * b_col
        lhs = jnp.concatenate([kb.astype(BF16), qn.astype(BF16)], axis=0)
        prod = lax.dot_general(lhs, kn.astype(BF16), NT, preferred_element_type=F32)
        lmat.append(jnp.where(strict, prod[:c] * decay, 0.0))
        intra.append(jnp.where(tri, prod[c:] * decay, 0.0).astype(BF16))
        eg = jnp.exp(g_col)
        qd.append((qn * eg).astype(BF16))
        kt.append((kn * jnp.exp(g_last - g_col)).astype(BF16))
        vbk.append((vn * b_col).astype(BF16))
        vbk.append((kb * eg).astype(BF16))
        cd.append(jnp.broadcast_to(jnp.exp(g_last), (1, d)))
    qd_ref[...] = jnp.concatenate(qd, axis=1)
    kt_ref[...] = jnp.concatenate(kt, axis=1)
    vbk_ref[...] = jnp.concatenate(vbk, axis=1)
    intra_ref[...] = jnp.concatenate(intra, axis=1)
    l_ref[...] = jnp.concatenate(lmat, axis=1)
    cd_ref[...] = jnp.concatenate(cd, axis=1)


def _solve_kernel(l_ref, t_ref):
    c, lanes = l_ref.shape[0], t_ref.shape[2]
    t_ref[...] = jnp.zeros_like(t_ref)
    for rb in range(c // SUBLANES):
        hi = SUBLANES * (rb + 1)
        col_id = lax.broadcasted_iota(jnp.int32, (hi, lanes), 0)

        def row(rr, carry, rb=rb, hi=hi, col_id=col_id):
            r = rb * SUBLANES + rr
            acc = jnp.where(col_id == r, 1.0, 0.0)
            for mb in range(rb + 1):
                w = SUBLANES * (mb + 1)
                terms = [l_ref[r, m:m + 1, :] * t_ref[m, 0:w, :] for m in range(mb * SUBLANES, w)]
                while len(terms) > 1:
                    terms = [a + b for a, b in zip(terms[::2], terms[1::2])]
                part = acc[:w] - terms[0]
                acc = part if w == hi else jnp.concatenate([part, acc[w:]], axis=0)
            t_ref[r, 0:hi, :] = acc
            return carry

        lax.fori_loop(0, SUBLANES, row, 0)


def _gdn_scan_kernel(qd_ref, kt_ref, vbk_ref, intra_ref, t_ref, cd_ref, z_ref, ng_ref, y_ref, st, *, heads):
    d, c = GDN_HEAD_DIM, CHUNK

    @pl.when(pl.program_id(1) == 0)
    def _():
        st[...] = jnp.zeros_like(st)

    ng = ng_ref[...]
    hs = range(heads)
    sl = [slice(h * d, (h + 1) * d) for h in hs]
    sc = [slice(h * c, (h + 1) * c) for h in hs]
    uw = [jnp.dot(t_ref[:, sc[h]].astype(BF16), vbk_ref[:, 2 * h * d:2 * (h + 1) * d],
                  preferred_element_type=F32) for h in hs]
    s = [st[h] for h in hs]
    ws_qs = [jnp.dot(jnp.concatenate([uw[h][:, d:].astype(BF16), qd_ref[:, sl[h]]], axis=0), s[h].astype(BF16),
                     preferred_element_type=F32) for h in hs]
    v16 = [(uw[h][:, :d] - ws_qs[h][:c]).astype(BF16) for h in hs]
    o = [ws_qs[h][c:] + jnp.dot(intra_ref[:, sc[h]], v16[h], preferred_element_type=F32) for h in hs]
    st[...] = jnp.stack([s[h] * cd_ref[:, sl[h]] + lax.dot_general(kt_ref[:, sl[h]], v16[h], TN,
                                                                     preferred_element_type=F32) for h in hs])
    o = [o[h] * lax.rsqrt(jnp.mean(o[h] * o[h], axis=-1, keepdims=True) + EPS) * ng for h in hs]
    y_ref[...] = (jnp.concatenate(o, axis=1) * _silu(z_ref[...])).astype(y_ref.dtype)


def _gated_deltanet(proj, ba, conv_w, a_log, dt_bias, norm_g, heads, width):
    b, s, _ = proj.shape
    d, c = GDN_HEAD_DIM, CHUNK
    mix = heads * d
    nchunks = s // c
    alog_l = jnp.pad(a_log, (heads, LANES - 2 * heads)).reshape(1, LANES)
    dtb_l = jnp.pad(dt_bias, (heads, LANES - 2 * heads)).reshape(1, LANES)

    tok = lambda width, blk=0: pl.BlockSpec((None, c, width), lambda bi, n: (bi, n, blk))
    per_chunk = pl.BlockSpec((None, None, 1, mix), lambda bi, n: (bi, n, 0, 0))
    whole = lambda arr: pl.BlockSpec(arr.shape, lambda bi, n: (0,) * arr.ndim)
    ts = lambda width, dt: jax.ShapeDtypeStruct((b, s, width), dt)

    qd, kt, vbk, intra, lmat, cd = pl.pallas_call(
        functools.partial(_gdn_prep_kernel, heads=heads),
        out_shape=(ts(mix, BF16), ts(mix, BF16), ts(2 * mix, BF16), ts(heads * c, BF16), ts(heads * c, F32),
                   jax.ShapeDtypeStruct((b, nchunks, 1, mix), F32)),
        grid=(b, nchunks),
        in_specs=[tok(3 * mix), whole(conv_w), tok(LANES), whole(alog_l), whole(dtb_l)],
        out_specs=(tok(mix), tok(mix), tok(2 * mix), tok(heads * c), tok(heads * c), per_chunk),
        scratch_shapes=[pltpu.VMEM((SUBLANES, 3 * mix), F32)],
        compiler_params=_params("parallel", "arbitrary"),
        name="gdn_prep",
    )(proj, conv_w, ba, alog_l, dtb_l)

    nb = b * nchunks
    nbp = -(-nb // LANES) * LANES
    l_t = jnp.pad(jnp.transpose(lmat.reshape(nb, c, heads * c), (1, 2, 0)), ((0, 0), (0, 0), (0, nbp - nb)))
    blk = pl.BlockSpec((c, c, LANES), lambda h, i: (0, h, i))
    t_t = pl.pallas_call(
        _solve_kernel,
        out_shape=jax.ShapeDtypeStruct((c, heads * c, nbp), F32),
        grid=(heads, nbp // LANES), in_specs=[blk], out_specs=blk,
        compiler_params=_params("parallel", "parallel"),
        name="gdn_solve",
    )(l_t)
    t_mat = jnp.transpose(t_t[:, :, :nb], (2, 0, 1)).reshape(b, s, heads * c)

    return pl.pallas_call(
        functools.partial(_gdn_scan_kernel, heads=heads),
        out_shape=ts(width, BF16),
        grid=(b, nchunks),
        in_specs=[tok(mix), tok(mix), tok(2 * mix), tok(heads * c), tok(heads * c), per_chunk,
                  tok(mix, 3), pl.BlockSpec((1, d), lambda bi, n: (0, 0))],
        out_specs=tok(mix),
        scratch_shapes=[pltpu.VMEM((heads, d, d), F32)],
        compiler_params=_params("parallel", "arbitrary"),
        name="gdn_scan",
    )(qd, kt, vbk, intra, t_mat, cd, proj, norm_g.reshape(1, d))


def kernel(x, mem, ffn_pre_g, ffn_post_g, mix_pre_g, mix_post_g, mem_g, ffn_w_gate, ffn_w_up, ffn_w_down,
           mem_w_kv, mix_w_out, sc_w_in, sc_conv_w, gdn_w_in, gdn_conv_w, gdn_a_log, gdn_dt_bias, gdn_norm_g):
    b, s, d = x.shape
    m = mem.shape[1]
    depth = ffn_pre_g.shape[0]
    xw = mem_w_kv.shape[-1] // 2
    mix = d - xw
    heads = mix // GDN_HEAD_DIM
    t = b * s
    gdn_w_in_t = jnp.swapaxes(gdn_w_in, 1, 2)

    h = x.reshape(t, d)
    mem2 = mem.reshape(b * m, d)
    pending = None

    def ffn(h, pending, i, half):
        h, _, hidden = _fused_in(pending, h, ffn_pre_g[i, half], [ffn_w_gate, ffn_w_up], (i, half),
                                 ffn_w_gate.shape[-1], BF16, "ffn_up", swiglu=True)
        y = _linear(hidden, ffn_w_down, (i, half), d, F32, "ffn_down")
        return h, (y, ffn_post_g[i, half], 0.5)

    for i in range(depth):
        j = i // N_MIXERS
        h, pending = ffn(h, pending, i, 0)

        mem_n = _norm_bf16(mem2, mem_g[i])
        kv = _linear(mem_n, mem_w_kv, (i,), 2 * xw, F32, "mem_kv").reshape(b, m, 2 * xw)
        if i % N_MIXERS == 0:
            h, _, proj = _fused_in(pending, h, mix_pre_g[i], [sc_w_in], (j,), 3 * mix + xw, F32, "sc_in")
            proj = proj.reshape(b, s, 3 * mix + xw)
            y = _short_conv(proj, sc_conv_w[j], mix, d)
            mixed_in = _cross_attention(proj, 3 * mix // xw, kv, xw, y)
        else:
            h, hn, proj = _fused_in(pending, h, mix_pre_g[i], [gdn_w_in_t], (j,), 4 * mix, F32, "gdn_in",
                                    w_transposed=True, emit_hn=True)
            w_tail = lax.slice(gdn_w_in_t, (j, 4 * mix, 0), (j + 1, gdn_w_in_t.shape[1], d))[0]
            w_ba = jnp.pad(w_tail[:2 * heads], ((0, LANES - 2 * heads), (0, 0)))
            ba = _linear(hn, w_ba, (), LANES, F32, "gdn_gates", w_transposed=True)
            xq = _linear(hn, w_tail[2 * heads:], (), xw, F32, "gdn_xq", w_transposed=True)
            y = _gated_deltanet(proj.reshape(b, s, 4 * mix), ba.reshape(b, s, LANES), gdn_conv_w[j],
                                gdn_a_log[j], gdn_dt_bias[j], gdn_norm_g[j], heads, d)
            mixed_in = _cross_attention(xq.reshape(b, s, xw), 0, kv, xw, y)
        mixed = _linear(mixed_in.reshape(t, d), mix_w_out, (i,), d, F32, "mix_out")
        pending = (mixed, mix_post_g[i], 1.0)

        h, pending = ffn(h, pending, i, 1)
    y, post_g, weight = pending
    return _residual(y, h, post_g, weight).reshape(b, s, d)
```

```python
import functools

import jax
import jax.numpy as jnp
from jax import lax
from jax.experimental import pallas as pl
from jax.experimental.pallas import tpu as pltpu

CHUNK = 64
N_MIXERS = 2
X_HEADS = 4
GDN_HEAD_DIM = 128
EPS = 1e-6
LANES = 128
SUBLANES = 8
V7X_VMEM_BYTES = 64 * 2**20
VMEM_LIMIT_BYTES = V7X_VMEM_BYTES - 6 * 2**20
LINEAR_BLOCK_BUDGET_BYTES = VMEM_LIMIT_BYTES - 8 * 2**20

F32 = jnp.float32
BF16 = jnp.bfloat16
NT = (((1,), (1,)), ((), ()))
TN = (((0,), (0,)), ((), ()))


def _tile(n, prefs):
    for p in prefs:
        if n % p == 0:
            return p
    return n


def _params(*semantics):
    return pltpu.CompilerParams(dimension_semantics=semantics, vmem_limit_bytes=VMEM_LIMIT_BYTES)


def _rms(x, g):
    return x * lax.rsqrt(jnp.mean(x * x, axis=-1, keepdims=True) + EPS) * g


def _silu(x):
    return x * jax.nn.sigmoid(x)


def _ordering_zero(v):
    bits = pltpu.bitcast(v.astype(F32), jnp.uint32)
    return ((bits >> 16) >> 16).astype(F32)


def _causal_conv(x, w, tail=None):
    width = w.shape[0]
    assert width - 1 <= SUBLANES
    row = lax.broadcasted_iota(jnp.int32, (SUBLANES, x.shape[1]), 0)
    acc = None
    for j in range(width):
        s = width - 1 - j
        if s == 0:
            xs = x
        else:
            rolled = pltpu.roll(x, s, 0)
            fill = 0.0 if tail is None else pltpu.roll(tail, s, 0)
            xs = jnp.concatenate([jnp.where(row >= s, rolled[:SUBLANES], fill), rolled[SUBLANES:]], axis=0)
        term = xs * w[j:j + 1, :]
        acc = term if acc is None else acc + term
    return acc


def _norm_kernel(x_ref, g_ref, o_ref):
    o_ref[...] = _rms(x_ref[...], g_ref[...]).astype(o_ref.dtype)


def _norm_bf16(x, g):
    t, d = x.shape
    tr = _tile(t, (256, 128, 64, 32, 16))
    return pl.pallas_call(
        _norm_kernel,
        out_shape=jax.ShapeDtypeStruct((t, d), BF16),
        grid=(t // tr,),
        in_specs=[pl.BlockSpec((tr, d), lambda i: (i, 0)), pl.BlockSpec((1, d), lambda i: (0, 0))],
        out_specs=pl.BlockSpec((tr, d), lambda i: (i, 0)),
        compiler_params=_params("parallel"),
        name="norm",
    )(x, g.reshape(1, d))


def _resid_kernel(y_ref, h_ref, pg_ref, h_out, *, weight):
    h_out[...] = h_ref[...] + weight * _rms(y_ref[...], pg_ref[...])


def _residual(y, h, post_g, weight):
    t, d = h.shape
    tr = _tile(t, (256, 128, 64, 32, 16))
    row = pl.BlockSpec((tr, d), lambda i: (i, 0))
    vec = pl.BlockSpec((1, d), lambda i: (0, 0))
    return pl.pallas_call(
        functools.partial(_resid_kernel, weight=weight),
        out_shape=jax.ShapeDtypeStruct((t, d), F32),
        grid=(t // tr,), in_specs=[row, row, vec], out_specs=row,
        compiler_params=_params("parallel"), name="residual",
    )(y, h, post_g.reshape(1, d))


def _linear_kernel(*refs, n_x, w_transposed):
    x_refs, w_ref, o_ref = refs[:n_x], refs[n_x], refs[n_x + 1]
    acc, k0 = None, 0
    for x_ref in x_refs:
        kk = x_ref.shape[1]
        if w_transposed:
            part = lax.dot_general(x_ref[...], w_ref[:, k0:k0 + kk].astype(BF16), NT, preferred_element_type=F32)
        else:
            part = jnp.dot(x_ref[...], w_ref[k0:k0 + kk, :].astype(BF16), preferred_element_type=F32)
        acc = part if acc is None else acc + part
        k0 += kk
    o_ref[...] = acc.astype(o_ref.dtype)


def _linear(xs, w, lead, ncols, out_dtype, name, w_transposed=False):
    t = xs[0].shape[0]
    k = sum(x.shape[1] for x in xs)
    assert w.shape[-1 if w_transposed else -2] == k
    for tm, tn in ((2048, 256), (1024, 512)):
        tm, tn = _tile(t, (tm, 1024, 512, 256, 128, 64, 32, 16)), _tile(ncols, (tn, 256, 128))
        if 2 * tm * k * 2 + k * tn * (2 * 4 + 2) + 2 * tm * tn * 4 <= LINEAR_BLOCK_BUDGET_BYTES:
            break
    squeezed = (None,) * len(lead)
    if w_transposed:
        w_spec = pl.BlockSpec(squeezed + (tn, k), lambda i, j: tuple(lead) + (j, 0))
    else:
        w_spec = pl.BlockSpec(squeezed + (k, tn), lambda i, j: tuple(lead) + (0, j))
    x_specs = [pl.BlockSpec((tm, x.shape[1]), lambda i, j: (i, 0)) for x in xs]
    return pl.pallas_call(
        functools.partial(_linear_kernel, n_x=len(xs), w_transposed=w_transposed),
        out_shape=jax.ShapeDtypeStruct((t, ncols), out_dtype),
        grid=(t // tm, ncols // tn),
        in_specs=x_specs + [w_spec],
        out_specs=pl.BlockSpec((tm, tn), lambda i, j: (i, j)),
        compiler_params=_params("parallel", "arbitrary"),
        name=name,
    )(*xs, w)


def _fused_in_kernel(*refs, has_y, weight, swiglu, w_transposed, emit_hn, n_chunks, rc):
    it = iter(refs)
    y_ref = next(it) if has_y else None
    h_ref = next(it)
    pg_ref = next(it) if has_y else None
    ng_ref = next(it)
    w_refs = [next(it) for _ in range(2 if swiglu else 1)]
    h_out = next(it) if has_y else None
    hn_out = next(it) if emit_hn else None
    o_ref = next(it)
    panels = (next(it), next(it))
    i, j = pl.program_id(0), pl.program_id(1)

    def chunk_work(panel):
        h = h_ref[...]
        if has_y:
            h = h + weight * _rms(y_ref[...], pg_ref[...])
            h_out[...] = h
        hn32 = _rms(h, ng_ref[...])
        hn = hn32.astype(BF16)
        if emit_hn:
            hn_out[...] = hn
        r0 = pl.multiple_of(jnp.minimum(j, n_chunks - 1) * rc, rc)
        panel[pl.ds(r0, rc), :] = hn
        tiles = [hn32[r:r + SUBLANES, c:c + LANES]
                 for r in range(0, rc, SUBLANES) for c in range(0, hn32.shape[1], LANES)]
        while len(tiles) > 1:
            tiles = [a + b for a, b in zip(tiles[::2], tiles[1::2])]
        return tiles[0]

    def bf16_weight(w_ref, chunk_digest):
        zero = _ordering_zero(chunk_digest)
        if w_transposed:
            mid = (w_ref.shape[1] // 2 // LANES) * LANES
            row = (w_ref.shape[0] // 2 // SUBLANES) * SUBLANES
            rows = slice(row, row + SUBLANES)
            w = jnp.concatenate([w_ref[rows, :mid], w_ref[rows, mid:mid + LANES] + zero, w_ref[rows, mid + LANES:]],
                                axis=1)
            return jnp.concatenate([w_ref[:row, :], w, w_ref[row + SUBLANES:, :]], axis=0).astype(BF16)
        mid = (w_ref.shape[0] // 2 // SUBLANES) * SUBLANES
        zero = jnp.concatenate([zero] * (w_ref.shape[1] // LANES), axis=1)
        return jnp.concatenate([w_ref[:mid, :], w_ref[mid:mid + SUBLANES, :] + zero, w_ref[mid + SUBLANES:, :]],
                               axis=0).astype(BF16)

    def matmul(panel, chunk_digest):
        x = panel[...]
        if swiglu:
            gate = jnp.dot(x, bf16_weight(w_refs[0], chunk_digest), preferred_element_type=F32)
            up = jnp.dot(x, bf16_weight(w_refs[1], chunk_digest), preferred_element_type=F32)
            out = _silu(gate) * up
        elif w_transposed:
            out = lax.dot_general(x, bf16_weight(w_refs[0], chunk_digest), NT, preferred_element_type=F32)
        else:
            out = jnp.dot(x, bf16_weight(w_refs[0], chunk_digest), preferred_element_type=F32)
        o_ref[...] = out.astype(o_ref.dtype)

    @pl.when(i == 0)
    def _():
        chunk_work(panels[0])

    for parity in (0, 1):
        @pl.when((i >= 1) & (i % 2 == parity))
        def _(parity=parity):
            matmul(panels[1 - parity], chunk_work(panels[parity]))


def _fused_in(pending, h, next_g, ws, lead, ncols, out_dtype, name, swiglu=False, w_transposed=False,
              emit_hn=False):
    t, d = h.shape
    has_y = pending is not None
    tm = _tile(t, (1024, 512, 256, 128, 64, 32, 16))
    tn = _tile(ncols, (256, 128)) if swiglu else _tile(ncols, (512, 256, 128))
    n_i, n_j = t // tm, ncols // tn
    rc = next(r for r in (64, 128, 256, 512, 1024) if tm % r == 0 and tm // r <= n_j)
    n_chunks = tm // rc

    def chunk_map(i, j):
        return (jnp.where(i < n_i, i * n_chunks + jnp.minimum(j, n_chunks - 1), n_i * n_chunks - 1), 0)

    def col(i, j):
        return jnp.where(i >= 1, j, 0)

    row = pl.BlockSpec((rc, d), chunk_map)
    vec = pl.BlockSpec((1, d), lambda i, j: (0, 0))
    squeezed = (None,) * len(lead)
    if w_transposed:
        w_spec = pl.BlockSpec(squeezed + (tn, d), lambda i, j: tuple(lead) + (col(i, j), 0))
    else:
        w_spec = pl.BlockSpec(squeezed + (d, tn), lambda i, j: tuple(lead) + (0, col(i, j)))
    out_spec = pl.BlockSpec((tm, tn), lambda i, j: (jnp.maximum(i - 1, 0), col(i, j)))

    args, in_specs, out_shape, out_specs = [], [], [], []
    if has_y:
        y, post_g, weight = pending
        args += [y, h, post_g.reshape(1, d)]
        in_specs += [row, row, vec]
        out_shape.append(jax.ShapeDtypeStruct((t, d), F32))
        out_specs.append(row)
    else:
        weight = None
        args.append(h)
        in_specs.append(row)
    args.append(next_g.reshape(1, d))
    in_specs.append(vec)
    args += list(ws)
    in_specs += [w_spec] * len(ws)
    if emit_hn:
        out_shape.append(jax.ShapeDtypeStruct((t, d), BF16))
        out_specs.append(row)
    out_shape.append(jax.ShapeDtypeStruct((t, ncols), out_dtype))
    out_specs.append(out_spec)

    res = list(pl.pallas_call(
        functools.partial(_fused_in_kernel, has_y=has_y, weight=weight, swiglu=swiglu, w_transposed=w_transposed,
                          emit_hn=emit_hn, n_chunks=n_chunks, rc=rc),
        out_shape=tuple(out_shape),
        grid=(n_i + 1, n_j),
        in_specs=in_specs,
        out_specs=tuple(out_specs),
        scratch_shapes=[pltpu.VMEM((tm, d), BF16)] * 2,
        compiler_params=_params("arbitrary", "arbitrary"),
        name=name,
    )(*args))
    h_new = res.pop(0) if has_y else h
    hn = res.pop(0) if emit_hn else None
    return h_new, hn, res[0]


def _sconv_kernel(b_ref, c_ref, h_ref, w_ref, o_ref):
    u = c_ref[...] * h_ref[...]
    o_ref[...] = (b_ref[...] * _causal_conv(u, w_ref[...])).astype(o_ref.dtype)


def _short_conv(proj, conv_w, mix):
    b, s, _ = proj.shape
    tc = _tile(mix, (256, 128))
    nb = mix // tc
    blk = lambda off: pl.BlockSpec((None, s, tc), lambda bi, j: (bi, 0, off + j))
    return pl.pallas_call(
        _sconv_kernel,
        out_shape=jax.ShapeDtypeStruct((b, s, mix), BF16),
        grid=(b, nb),
        in_specs=[blk(0), blk(nb), blk(2 * nb), pl.BlockSpec((conv_w.shape[0], tc), lambda bi, j: (0, j))],
        out_specs=pl.BlockSpec((None, s, tc), lambda bi, j: (bi, 0, j)),
        compiler_params=_params("parallel", "parallel"),
        name="short_conv",
    )(proj, proj, proj, conv_w)


def _xattn_kernel(q_ref, k_ref, v_ref, o_ref):
    dh = q_ref.shape[-1] // X_HEADS
    scale = dh ** -0.5
    for hd in range(X_HEADS):
        sl = slice(hd * dh, (hd + 1) * dh)
        q = q_ref[:, sl].astype(BF16)
        k = k_ref[:, sl].astype(BF16)
        v = v_ref[:, sl].astype(BF16)
        s = lax.dot_general(q, k, NT, preferred_element_type=F32) * scale
        e = jnp.exp(s - jnp.max(s, axis=-1, keepdims=True))
        p = e / jnp.sum(e, axis=-1, keepdims=True)
        o_ref[:, sl] = jnp.dot(p.astype(BF16), v, preferred_element_type=F32).astype(o_ref.dtype)


def _cross_attention(q_arr, q_blk, kv, xw):
    b, s, _ = q_arr.shape
    m = kv.shape[1]
    tq = _tile(s, (512, 256, 128, 64))
    return pl.pallas_call(
        _xattn_kernel,
        out_shape=jax.ShapeDtypeStruct((b, s, xw), BF16),
        grid=(b, s // tq),
        in_specs=[pl.BlockSpec((None, tq, xw), lambda bi, i: (bi, i, q_blk)),
                  pl.BlockSpec((None, m, xw), lambda bi, i: (bi, 0, 0)),
                  pl.BlockSpec((None, m, xw), lambda bi, i: (bi, 0, 1))],
        out_specs=pl.BlockSpec((None, tq, xw), lambda bi, i: (bi, i, 0)),
        compiler_params=_params("parallel", "parallel"),
        name="cross_attention",
    )(q_arr, kv, kv)


def _gdn_prep_kernel(qkv_ref, cw_ref, ba_ref, alog_ref, dtb_ref,
                     qd_ref, kt_ref, vbk_ref, intra_ref, l_ref, cd_ref, prev_s, *, heads):
    d, c = GDN_HEAD_DIM, CHUNK
    mix = heads * d

    @pl.when(pl.program_id(1) == 0)
    def _():
        prev_s[...] = jnp.zeros_like(prev_s)

    cur = qkv_ref[...]
    x = _silu(_causal_conv(cur, cw_ref[...], prev_s[...]))
    prev_s[...] = cur[c - SUBLANES:, :]

    ba = ba_ref[...]
    beta = jax.nn.sigmoid(ba)
    xg = ba + dtb_ref[...]
    softplus = jnp.maximum(xg, 0.0) + jnp.log1p(jnp.exp(-jnp.abs(xg)))
    g = -jnp.exp(alog_ref[...]) * softplus
    ii = lax.broadcasted_iota(jnp.int32, (c, c), 0)
    jj = lax.broadcasted_iota(jnp.int32, (c, c), 1)
    tri, strict = ii >= jj, ii > jj
    gc = jnp.dot(tri.astype(F32), g, precision=lax.Precision.HIGHEST, preferred_element_type=F32)
    gc_t = gc.T
    lane = lax.broadcasted_iota(jnp.int32, ba.shape, 1)

    def column(arr, idx):
        return jnp.sum(jnp.where(lane == idx, arr, 0.0), axis=1, keepdims=True)

    def l2n(v):
        return v * lax.rsqrt(jnp.sum(v * v, axis=-1, keepdims=True) + EPS)

    qd, kt, vbk, intra, lmat, cd = [], [], [], [], [], []
    for h in range(heads):
        qn = l2n(x[:, h * d:(h + 1) * d]) * (d ** -0.5)
        kn = l2n(x[:, mix + h * d:mix + (h + 1) * d])
        vn = x[:, 2 * mix + h * d:2 * mix + (h + 1) * d]
        g_col = column(gc, heads + h)
        b_col = column(beta, h)
        g_row = gc_t[heads + h:heads + h + 1, :]
        g_last = g_row[:, c - 1:c]
        decay = jnp.where(tri, jnp.exp(jnp.where(tri, g_col - g_row, 0.0)), 0.0)
        kb = kn * b_col
        lhs = jnp.concatenate([kb.astype(BF16), qn.astype(BF16)], axis=0)
        prod = lax.dot_general(lhs, kn.astype(BF16), NT, preferred_element_type=F32)
        lmat.append(jnp.where(strict, prod[:c] * decay, 0.0))
        intra.append(jnp.where(tri, prod[c:] * decay, 0.0).astype(BF16))
        eg = jnp.exp(g_col)
        qd.append((qn * eg).astype(BF16))
        kt.append((kn * jnp.exp(g_last - g_col)).astype(BF16))
        vbk.append((vn * b_col).astype(BF16))
        vbk.append((kb * eg).astype(BF16))
        cd.append(jnp.broadcast_to(jnp.exp(g_last), (1, d)))
    qd_ref[...] = jnp.concatenate(qd, axis=1)
    kt_ref[...] = jnp.concatenate(kt, axis=1)
    vbk_ref[...] = jnp.concatenate(vbk, axis=1)
    intra_ref[...] = jnp.concatenate(intra, axis=1)
    l_ref[...] = jnp.concatenate(lmat, axis=1)
    cd_ref[...] = jnp.concatenate(cd, axis=1)


def _solve_kernel(l_ref, t_ref):
    c, lanes = l_ref.shape[0], t_ref.shape[2]
    t_ref[...] = jnp.zeros_like(t_ref)
    for rb in range(c // SUBLANES):
        hi = SUBLANES * (rb + 1)
        col_id = lax.broadcasted_iota(jnp.int32, (hi, lanes), 0)

        def row(rr, carry, rb=rb, hi=hi, col_id=col_id):
            r = rb * SUBLANES + rr
            acc = jnp.where(col_id == r, 1.0, 0.0)
            for mb in range(rb + 1):
                w = SUBLANES * (mb + 1)
                terms = [l_ref[r, m:m + 1, :] * t_ref[m, 0:w, :] for m in range(mb * SUBLANES, w)]
                while len(terms) > 1:
                    terms = [a + b for a, b in zip(terms[::2], terms[1::2])]
                part = acc[:w] - terms[0]
                acc = part if w == hi else jnp.concatenate([part, acc[w:]], axis=0)
            t_ref[r, 0:hi, :] = acc
            return carry

        lax.fori_loop(0, SUBLANES, row, 0)


def _gdn_scan_kernel(qd_ref, kt_ref, vbk_ref, intra_ref, t_ref, cd_ref, z_ref, ng_ref, y_ref, st, *, heads):
    d, c = GDN_HEAD_DIM, CHUNK

    @pl.when(pl.program_id(1) == 0)
    def _():
        st[...] = jnp.zeros_like(st)

    ng = ng_ref[...]
    hs = range(heads)
    sl = [slice(h * d, (h + 1) * d) for h in hs]
    sc = [slice(h * c, (h + 1) * c) for h in hs]
    uw = [jnp.dot(t_ref[:, sc[h]].astype(BF16), vbk_ref[:, 2 * h * d:2 * (h + 1) * d],
                  preferred_element_type=F32) for h in hs]
    s = [st[h] for h in hs]
    ws_qs = [jnp.dot(jnp.concatenate([uw[h][:, d:].astype(BF16), qd_ref[:, sl[h]]], axis=0), s[h].astype(BF16),
                     preferred_element_type=F32) for h in hs]
    v16 = [(uw[h][:, :d] - ws_qs[h][:c]).astype(BF16) for h in hs]
    o = [ws_qs[h][c:] + jnp.dot(intra_ref[:, sc[h]], v16[h], preferred_element_type=F32) for h in hs]
    st[...] = jnp.stack([s[h] * cd_ref[:, sl[h]] + lax.dot_general(kt_ref[:, sl[h]], v16[h], TN,
                                                                     preferred_element_type=F32) for h in hs])
    o = [o[h] * lax.rsqrt(jnp.mean(o[h] * o[h], axis=-1, keepdims=True) + EPS) * ng for h in hs]
    y_ref[...] = (jnp.concatenate(o, axis=1) * _silu(z_ref[...])).astype(y_ref.dtype)


def _gated_deltanet(proj, ba, conv_w, a_log, dt_bias, norm_g, heads):
    b, s, _ = proj.shape
    d, c = GDN_HEAD_DIM, CHUNK
    mix = heads * d
    nchunks = s // c
    alog_l = jnp.pad(a_log, (heads, LANES - 2 * heads)).reshape(1, LANES)
    dtb_l = jnp.pad(dt_bias, (heads, LANES - 2 * heads)).reshape(1, LANES)

    tok = lambda width, blk=0: pl.BlockSpec((None, c, width), lambda bi, n: (bi, n, blk))
    per_chunk = pl.BlockSpec((None, None, 1, mix), lambda bi, n: (bi, n, 0, 0))
    whole = lambda arr: pl.BlockSpec(arr.shape, lambda bi, n: (0,) * arr.ndim)
    ts = lambda width, dt: jax.ShapeDtypeStruct((b, s, width), dt)

    qd, kt, vbk, intra, lmat, cd = pl.pallas_call(
        functools.partial(_gdn_prep_kernel, heads=heads),
        out_shape=(ts(mix, BF16), ts(mix, BF16), ts(2 * mix, BF16), ts(heads * c, BF16), ts(heads * c, F32),
                   jax.ShapeDtypeStruct((b, nchunks, 1, mix), F32)),
        grid=(b, nchunks),
        in_specs=[tok(3 * mix), whole(conv_w), tok(LANES), whole(alog_l), whole(dtb_l)],
        out_specs=(tok(mix), tok(mix), tok(2 * mix), tok(heads * c), tok(heads * c), per_chunk),
        scratch_shapes=[pltpu.VMEM((SUBLANES, 3 * mix), F32)],
        compiler_params=_params("parallel", "arbitrary"),
        name="gdn_prep",
    )(proj, conv_w, ba, alog_l, dtb_l)

    nb = b * nchunks
    nbp = -(-nb // LANES) * LANES
    l_t = jnp.pad(jnp.transpose(lmat.reshape(nb, c, heads * c), (1, 2, 0)), ((0, 0), (0, 0), (0, nbp - nb)))
    blk = pl.BlockSpec((c, c, LANES), lambda h, i: (0, h, i))
    t_t = pl.pallas_call(
        _solve_kernel,
        out_shape=jax.ShapeDtypeStruct((c, heads * c, nbp), F32),
        grid=(heads, nbp // LANES), in_specs=[blk], out_specs=blk,
        compiler_params=_params("parallel", "parallel"),
        name="gdn_solve",
    )(l_t)
    t_mat = jnp.transpose(t_t[:, :, :nb], (2, 0, 1)).reshape(b, s, heads * c)

    return pl.pallas_call(
        functools.partial(_gdn_scan_kernel, heads=heads),
        out_shape=ts(mix, BF16),
        grid=(b, nchunks),
        in_specs=[tok(mix), tok(mix), tok(2 * mix), tok(heads * c), tok(heads * c), per_chunk,
                  tok(mix, 3), pl.BlockSpec((1, d), lambda bi, n: (0, 0))],
        out_specs=tok(mix),
        scratch_shapes=[pltpu.VMEM((heads, d, d), F32)],
        compiler_params=_params("parallel", "arbitrary"),
        name="gdn_scan",
    )(qd, kt, vbk, intra, t_mat, cd, proj, norm_g.reshape(1, d))


def kernel(x, mem, ffn_pre_g, ffn_post_g, mix_pre_g, mix_post_g, mem_g, ffn_w_gate, ffn_w_up, ffn_w_down,
           mem_w_kv, mix_w_out, sc_w_in, sc_conv_w, gdn_w_in, gdn_conv_w, gdn_a_log, gdn_dt_bias, gdn_norm_g):
    b, s, d = x.shape
    m = mem.shape[1]
    depth = ffn_pre_g.shape[0]
    xw = mem_w_kv.shape[-1] // 2
    mix = d - xw
    heads = mix // GDN_HEAD_DIM
    t = b * s
    gdn_w_in_t = jnp.swapaxes(gdn_w_in, 1, 2)

    h = x.reshape(t, d)
    mem2 = mem.reshape(b * m, d)
    pending = None

    def ffn(h, pending, i, half):
        h, _, hidden = _fused_in(pending, h, ffn_pre_g[i, half], [ffn_w_gate, ffn_w_up], (i, half),
                                 ffn_w_gate.shape[-1], BF16, "ffn_up", swiglu=True)
        y = _linear([hidden], ffn_w_down, (i, half), d, F32, "ffn_down")
        return h, (y, ffn_post_g[i, half], 0.5)

    for i in range(depth):
        j = i // N_MIXERS
        h, pending = ffn(h, pending, i, 0)

        mem_n = _norm_bf16(mem2, mem_g[i])
        kv = _linear([mem_n], mem_w_kv, (i,), 2 * xw, F32, "mem_kv").reshape(b, m, 2 * xw)
        if i % N_MIXERS == 0:
            h, _, proj = _fused_in(pending, h, mix_pre_g[i], [sc_w_in], (j,), 3 * mix + xw, F32, "sc_in")
            proj = proj.reshape(b, s, 3 * mix + xw)
            y = _short_conv(proj, sc_conv_w[j], mix)
            xo = _cross_attention(proj, 3 * mix // xw, kv, xw)
        else:
            h, hn, proj = _fused_in(pending, h, mix_pre_g[i], [gdn_w_in_t], (j,), 4 * mix, F32, "gdn_in",
                                    w_transposed=True, emit_hn=True)
            w_tail = lax.slice(gdn_w_in_t, (j, 4 * mix, 0), (j + 1, gdn_w_in_t.shape[1], d))[0]
            w_ba = jnp.pad(w_tail[:2 * heads], ((0, LANES - 2 * heads), (0, 0)))
            ba = _linear([hn], w_ba, (), LANES, F32, "gdn_gates", w_transposed=True)
            xq = _linear([hn], w_tail[2 * heads:], (), xw, F32, "gdn_xq", w_transposed=True)
            y = _gated_deltanet(proj.reshape(b, s, 4 * mix), ba.reshape(b, s, LANES), gdn_conv_w[j],
                                gdn_a_log[j], gdn_dt_bias[j], gdn_norm_g[j], heads)
            xo = _cross_attention(xq.reshape(b, s, xw), 0, kv, xw)
        mixed = _linear([y.reshape(t, mix), xo.reshape(t, xw)], mix_w_out, (i,), d, F32, "mix_out")
        pending = (mixed, mix_post_g[i], 1.0)

        h, pending = ffn(h, pending, i, 1)
    y, post_g, weight = pending
    return _residual(y, h, post_g, weight).reshape(b, s, d)
```
